```python
import jax
import jax.numpy as jnp
from jax import lax
import numpy as np

D_MODEL = 1024
BATCH = 16
SEQ = 4096
DEPTH = 4

HEAD_DIM = 64
MIX_WIDTH = D_MODEL
RWKV_WIDTH = 3 * D_MODEL // 8
LRU_WIDTH = D_MODEL // 4
SBA_WIDTH = MIX_WIDTH - RWKV_WIDTH - LRU_WIDTH
RWKV_HEADS = RWKV_WIDTH // HEAD_DIM
LRU_BLOCKS = LRU_WIDTH // HEAD_DIM
SBA_HEADS = SBA_WIDTH // HEAD_DIM

DECAY_RANK = 64
ICLR_RANK = 64
GATE_RANK = 128
RWKV_GN_EPS = 64e-5

CONV_WIDTH = 4
LRU_C = 8.0

SBA_BLOCK = 128

RWKV_COLS = 3 * RWKV_WIDTH + DECAY_RANK + ICLR_RANK + GATE_RANK
LRU_COLS = 2 * LRU_WIDTH
SBA_COLS = 3 * SBA_WIDTH
IN_COLS = RWKV_COLS + LRU_COLS + SBA_COLS

N_EXPERTS = 16
N_GROUPS = 4
EXPERTS_PER_GROUP = N_EXPERTS // N_GROUPS
TOP_K = 2
D_EXPERT = 512
MOE_BLOCK = 256

ALPHA = (2.0 * DEPTH) ** 0.25
BETA = (8.0 * DEPTH) ** -0.25
LN_EPS = 1e-5
RMS_EPS = 1e-6

kernel_name = 'hymba_rwkv7_rglru_stickbreak_grouped_moe'


def layer_norm(x, g, b):
    xf = x.astype(jnp.float32)
    mu = jnp.mean(xf, -1, keepdims=True)
    var = jnp.mean(jnp.square(xf - mu), -1, keepdims=True)
    return ((xf - mu) * lax.rsqrt(var + LN_EPS) * g + b).astype(x.dtype)


def head_rms_norm(y, g, n_heads):
    bsz, seq, c = y.shape
    yh = y.reshape(bsz, seq, n_heads, c // n_heads)
    yh = yh * lax.rsqrt(jnp.mean(jnp.square(yh), -1, keepdims=True) + RMS_EPS)
    return yh.reshape(bsz, seq, c) * g


def token_shift(p, mu):
    prev = jnp.pad(p, ((0, 0), (1, 0), (0, 0)))[:, :-1]
    return p + mu * (prev - p)


def rwkv7_scan(r, w, k, v, kk, a):
    bsz, _, h, n = r.shape

    def step(state, inp):
        r_t, w_t, k_t, v_t, kk_t, a_t = inp
        sa = jnp.einsum('bhvk,bhk->bhv', state, -kk_t)
        state = (state * w_t[:, :, None, :]
                 + sa[..., None] * (kk_t * a_t)[:, :, None, :]
                 + v_t[..., None] * k_t[:, :, None, :])
        return state, jnp.einsum('bhvk,bhk->bhv', state, r_t)

    xs = tuple(jnp.moveaxis(t, 1, 0) for t in (r, w, k, v, kk, a))
    state0 = jnp.zeros((bsz, h, n, n), jnp.float32)
    _, o = lax.scan(step, state0, xs)
    return jnp.moveaxis(o, 0, 1)


def rwkv7_time_mix(p, w0, w_up, a0, a_up, g_up, k_k, k_a, r_k, ln_g, ln_b):
    bsz, seq, _ = p.shape
    c = RWKV_WIDTH
    r, k, v, dw, da, dg = jnp.split(
        p, [c, 2 * c, 3 * c, 3 * c + DECAY_RANK, 3 * c + DECAY_RANK + ICLR_RANK], axis=-1)
    w_log = -jax.nn.softplus(-(w0 + jnp.tanh(dw) @ w_up)) - 0.5
    decay = jnp.exp(-jnp.exp(w_log))
    a = jax.nn.sigmoid(a0 + da @ a_up)
    g = jax.nn.sigmoid(dg) @ g_up

    def heads(t):
        return t.reshape(bsz, seq, RWKV_HEADS, HEAD_DIM)

    kk = heads(k * k_k)
    kk = kk / jnp.maximum(jnp.sqrt(jnp.sum(kk * kk, -1, keepdims=True)), 1e-12)
    k = k * (1.0 + (a - 1.0) * k_a)
    rh, kh, vh = heads(r), heads(k), heads(v)
    o = rwkv7_scan(rh, heads(decay), kh, vh, kk, heads(a))
    mu = jnp.mean(o, -1, keepdims=True)
    var = jnp.mean(jnp.square(o - mu), -1, keepdims=True)
    o = ((o - mu) * lax.rsqrt(var + RWKV_GN_EPS)).reshape(bsz, seq, c) * ln_g + ln_b
    bonus = jnp.sum(rh * kh * r_k.reshape(RWKV_HEADS, HEAD_DIM), -1, keepdims=True) * vh
    return (o + bonus.reshape(bsz, seq, c)) * g


def rglru_mix(p, conv_w, conv_b, wa, ba, wx, bx, lam, norm_g):
    bsz, seq, _ = p.shape
    gate_in, xin = jnp.split(p, [LRU_WIDTH], axis=-1)
    xc = lax.conv_general_dilated(
        xin, conv_w.astype(jnp.float32)[:, None, :], window_strides=(1,),
        padding=((CONV_WIDTH - 1, 0),), dimension_numbers=('NWC', 'WIO', 'NWC'),
        feature_group_count=LRU_WIDTH) + conv_b
    xb = xc.reshape(bsz, seq, LRU_BLOCKS, HEAD_DIM)
    gate_a = jax.nn.sigmoid(jnp.einsum('bsgi,gij->bsgj', xb, wa).reshape(bsz, seq, LRU_WIDTH) + ba)
    gate_x = jax.nn.sigmoid(jnp.einsum('bsgi,gij->bsgj', xb, wx).reshape(bsz, seq, LRU_WIDTH) + bx)
    log_a = -LRU_C * gate_a * jax.nn.softplus(-lam)
    a = jnp.exp(log_a)
    b_in = jnp.sqrt(-jnp.expm1(2.0 * log_a)) * (gate_x * xc)

    def combine(left, right):
        a1, b1 = left
        a2, b2 = right
        return a1 * a2, a2 * b1 + b2

    _, h = lax.associative_scan(combine, (a, b_in), axis=1)
    y = jax.nn.gelu(gate_in) * h
    return head_rms_norm(y, norm_g, LRU_BLOCKS)


def stick_breaking_attention(p, norm_g):
    bsz, seq, _ = p.shape
    q, k, v = jnp.split(p, [SBA_WIDTH, 2 * SBA_WIDTH], axis=-1)

    def heads(t):
        return jnp.transpose(t.reshape(bsz, seq, SBA_HEADS, HEAD_DIM), (0, 2, 1, 3))

    q, k, v = heads(q), heads(k), heads(v)
    scale = HEAD_DIM ** -0.5
    outs = []
    for i in range(seq // SBA_BLOCK):
        q0 = i * SBA_BLOCK
        kv_len = q0 + SBA_BLOCK
        z = jnp.einsum('bhtd,bhsd->bhts', q[:, :, q0:kv_len], k[:, :, :kv_len]) * scale
        t_idx = q0 + jnp.arange(SBA_BLOCK)[:, None]
        s_idx = jnp.arange(kv_len)[None, :]
        mask = s_idx < t_idx
        log_keep = jnp.where(mask, jax.nn.log_sigmoid(-z), 0.0)
        after = lax.cumsum(log_keep, axis=3, reverse=True) - log_keep
        attn = jnp.where(mask, jnp.exp(jax.nn.log_sigmoid(z) + after), 0.0)
        outs.append(jnp.einsum('bhts,bhsd->bhtd', attn, v[:, :, :kv_len]))
    o = jnp.concatenate(outs, axis=2)
    o = jnp.transpose(o, (0, 2, 1, 3)).reshape(bsz, seq, SBA_WIDTH)
    return head_rms_norm(o, norm_g, SBA_HEADS)


def grouped_moe(x2d, router_w, router_b, w_gate, w_up, w_down):
    n_tok, d = x2d.shape
    scores = jax.nn.softmax(x2d.astype(jnp.float32) @ router_w.astype(jnp.float32) + router_b, axis=-1)
    grouped = scores.reshape(n_tok, N_GROUPS, EXPERTS_PER_GROUP)
    group_score = jnp.sum(lax.top_k(grouped, TOP_K)[0], -1)
    sel = jnp.argmax(group_score, -1).astype(jnp.int32)
    cand = jnp.take_along_axis(grouped, sel[:, None, None], axis=1)[:, 0]
    top_vals, top_idx = lax.top_k(cand, TOP_K)
    expert_ids = sel[:, None] * EXPERTS_PER_GROUP + top_idx.astype(jnp.int32)
    gates = top_vals / jnp.sum(top_vals, -1, keepdims=True)

    n_assign = n_tok * TOP_K
    flat_e = expert_ids.reshape(n_assign)
    flat_tok = jnp.repeat(jnp.arange(n_tok, dtype=jnp.int32), TOP_K)
    flat_g = gates.reshape(n_assign)
    order = jnp.argsort(flat_e)
    se = flat_e[order]
    counts = jnp.bincount(flat_e, length=N_EXPERTS)
    starts = jnp.cumsum(counts) - counts
    padded = (counts + MOE_BLOCK - 1) // MOE_BLOCK * MOE_BLOCK
    pad_ends = jnp.cumsum(padded)
    pad_starts = pad_ends - padded
    dest = pad_starts[se] + jnp.arange(n_assign, dtype=jnp.int32) - starts[se]
    cap = n_assign + N_EXPERTS * MOE_BLOCK
    n_blocks = cap // MOE_BLOCK
    buf_tok = jnp.zeros((cap,), jnp.int32).at[dest].set(flat_tok[order])
    buf_gate = jnp.zeros((cap,), jnp.float32).at[dest].set(flat_g[order])
    block_e = jnp.minimum(
        jnp.searchsorted(pad_ends, jnp.arange(n_blocks, dtype=jnp.int32) * MOE_BLOCK, side='right'),
        N_EXPERTS - 1)

    def expert_block(acc, blk):
        tok, e, g = blk
        xb = x2d[tok]
        h = jax.nn.silu(xb @ w_gate[e]) * (xb @ w_up[e])
        yb = (h @ w_down[e]) * g[:, None]
        return acc.at[tok].add(yb.astype(acc.dtype)), None

    acc, _ = lax.scan(expert_block, jnp.zeros_like(x2d),
                      (buf_tok.reshape(n_blocks, MOE_BLOCK), block_e,
                       buf_gate.reshape(n_blocks, MOE_BLOCK)))
    return acc


def setup_inputs(seed: int = 0) -> dict:
    key = jax.random.key(seed)
    ks = jax.random.split(key, 40)
    f32 = jnp.float32
    L = DEPTH

    def nrm(k, shape, scale):
        return jax.random.normal(k, shape, f32) * scale

    u = jax.random.uniform(ks[20], (L, LRU_WIDTH), f32, 0.9, 0.999)
    p_lam = u ** (1.0 / LRU_C)
    return {
        'x': nrm(ks[0], (BATCH, SEQ, D_MODEL), 1.0),
        'w_in': nrm(ks[1], (L, D_MODEL, IN_COLS), D_MODEL ** -0.5),
        'shift_mu': jax.random.uniform(ks[2], (L, RWKV_COLS), f32),
        'rwkv_w0': jax.random.uniform(ks[3], (L, RWKV_WIDTH), f32, -6.0, 1.0),
        'rwkv_w_up': nrm(ks[4], (L, DECAY_RANK, RWKV_WIDTH), 0.5 * DECAY_RANK ** -0.5),
        'rwkv_a0': nrm(ks[5], (L, RWKV_WIDTH), 0.5),
        'rwkv_a_up': nrm(ks[6], (L, ICLR_RANK, RWKV_WIDTH), 0.5 * ICLR_RANK ** -0.5),
        'rwkv_g_up': nrm(ks[7], (L, GATE_RANK, RWKV_WIDTH), GATE_RANK ** -0.5),
        'rwkv_k_k': 0.85 + nrm(ks[8], (L, RWKV_WIDTH), 0.02),
        'rwkv_k_a': 1.0 + nrm(ks[9], (L, RWKV_WIDTH), 0.02),
        'rwkv_r_k': nrm(ks[10], (L, RWKV_WIDTH), 0.1),
        'rwkv_ln_g': 1.0 + nrm(ks[11], (L, RWKV_WIDTH), 0.02),
        'rwkv_ln_b': nrm(ks[12], (L, RWKV_WIDTH), 0.02),
        'lru_conv_w': nrm(ks[13], (L, CONV_WIDTH, LRU_WIDTH), CONV_WIDTH ** -0.5),
        'lru_conv_b': nrm(ks[14], (L, LRU_WIDTH), 0.02),
        'lru_wa': nrm(ks[15], (L, LRU_BLOCKS, HEAD_DIM, HEAD_DIM), HEAD_DIM ** -0.5),
        'lru_ba': nrm(ks[16], (L, LRU_WIDTH), 0.02),
        'lru_wx': nrm(ks[17], (L, LRU_BLOCKS, HEAD_DIM, HEAD_DIM), HEAD_DIM ** -0.5),
        'lru_bx': nrm(ks[18], (L, LRU_WIDTH), 0.02),
        'lru_lambda': jnp.log(p_lam) - jnp.log1p(-p_lam),
        'lru_norm_g': 1.0 + nrm(ks[19], (L, LRU_WIDTH), 0.02),
        'sba_norm_g': 1.0 + nrm(ks[21], (L, SBA_WIDTH), 0.02),
        'w_out': nrm(ks[22], (L, MIX_WIDTH, D_MODEL), MIX_WIDTH ** -0.5 * BETA),
        'ln1_g': 1.0 + nrm(ks[23], (L, D_MODEL), 0.02),
        'ln1_b': nrm(ks[24], (L, D_MODEL), 0.02),
        'ln2_g': 1.0 + nrm(ks[25], (L, D_MODEL), 0.02),
        'ln2_b': nrm(ks[26], (L, D_MODEL), 0.02),
        'router_w': nrm(ks[27], (D_MODEL, N_EXPERTS), D_MODEL ** -0.5),
        'router_b': nrm(ks[28], (N_EXPERTS,), 0.01),
        'exp_w_gate': nrm(ks[29], (L, N_EXPERTS, D_MODEL, D_EXPERT), D_MODEL ** -0.5),
        'exp_w_up': nrm(ks[30], (L, N_EXPERTS, D_MODEL, D_EXPERT), D_MODEL ** -0.5),
        'exp_w_down': nrm(ks[31], (L, N_EXPERTS, D_EXPERT, D_MODEL), D_EXPERT ** -0.5 * BETA),
    }


def reference(x, w_in, shift_mu, rwkv_w0, rwkv_w_up, rwkv_a0, rwkv_a_up, rwkv_g_up,
              rwkv_k_k, rwkv_k_a, rwkv_r_k, rwkv_ln_g, rwkv_ln_b, lru_conv_w, lru_conv_b,
              lru_wa, lru_ba, lru_wx, lru_bx, lru_lambda, lru_norm_g, sba_norm_g, w_out,
              ln1_g, ln1_b, ln2_g, ln2_b, router_w, router_b, exp_w_gate, exp_w_up,
              exp_w_down):
    bsz, seq, d = x.shape
    f32 = jnp.float32
    for l in range(DEPTH):
        p = x @ w_in[l]
        p_rwkv = token_shift(p[..., :RWKV_COLS].astype(f32), shift_mu[l])
        y_rwkv = rwkv7_time_mix(p_rwkv, rwkv_w0[l], rwkv_w_up[l], rwkv_a0[l], rwkv_a_up[l],
                                rwkv_g_up[l], rwkv_k_k[l], rwkv_k_a[l], rwkv_r_k[l],
                                rwkv_ln_g[l], rwkv_ln_b[l])
        y_lru = rglru_mix(p[..., RWKV_COLS:RWKV_COLS + LRU_COLS].astype(f32), lru_conv_w[l],
                          lru_conv_b[l], lru_wa[l], lru_ba[l], lru_wx[l], lru_bx[l],
                          lru_lambda[l], lru_norm_g[l])
        y_sba = stick_breaking_attention(p[..., RWKV_COLS + LRU_COLS:].astype(f32), sba_norm_g[l])
        mix = jnp.concatenate([y_rwkv, y_lru, y_sba], axis=-1).astype(x.dtype) @ w_out[l]
        x = layer_norm(ALPHA * x + mix, ln1_g[l], ln1_b[l])
        moe = grouped_moe(x.reshape(bsz * seq, d), router_w, router_b, exp_w_gate[l],
                          exp_w_up[l], exp_w_down[l]).reshape(bsz, seq, d)
        x = layer_norm(ALPHA * x + moe, ln2_g[l], ln2_b[l])
    return x
```

```python
import functools

import jax
import jax.numpy as jnp
from jax import lax
from jax.experimental import pallas as pl
from jax.experimental.pallas import tpu as pltpu

F32 = jnp.float32
BF16 = jnp.bfloat16

HEAD_DIM = 64
RWKV_WIDTH = 384
LRU_WIDTH = 256
SBA_WIDTH = 384
DECAY_RANK = 64
ICLR_RANK = 64
GATE_RANK = 128
RWKV_COLS = 3 * RWKV_WIDTH + DECAY_RANK + ICLR_RANK + GATE_RANK
LRU_COLS = 2 * LRU_WIDTH
SBA_COLS = 3 * SBA_WIDTH
RWKV_GN_EPS = 64e-5
CONV_WIDTH = 4
LRU_C = 8.0
N_EXPERTS = 16
N_GROUPS = 4
EXPERTS_PER_GROUP = 4
DEPTH = 4
ALPHA = (2.0 * DEPTH) ** 0.25
LN_EPS = 1e-5
RMS_EPS = 1e-6

LANES = 128
RWKV_CHUNK = 64
VMEM_LIMIT = 48 * 1024 * 1024


def _dot(a, b):
    return jnp.dot(a, b, preferred_element_type=F32)


def _dot_nt(a, b):
    return lax.dot_general(a, b, (((1,), (1,)), ((), ())), preferred_element_type=F32)


def _dot_tn(a, b):
    return lax.dot_general(a, b, (((0,), (0,)), ((), ())), preferred_element_type=F32)


def _split3(x):
    hi = x.astype(BF16)
    r1 = x - hi.astype(F32)
    mid = r1.astype(BF16)
    lo = (r1 - mid.astype(F32)).astype(BF16)
    return hi, mid, lo


def _dot_sel_rhs(x, sel):
    hi, mid, lo = _split3(x)
    return _dot(hi, sel) + _dot(mid, sel) + _dot(lo, sel)


def _dot_sel_lhs(sel, x):
    hi, mid, lo = _split3(x)
    return _dot(sel, hi) + _dot(sel, mid) + _dot(sel, lo)


def _dot3(a, b, dot=_dot):
    ah, am, al = _split3(a)
    bh, bm, bl = _split3(b)
    return (dot(ah, bh) + (dot(ah, bm) + dot(am, bh))
            + (dot(ah, bl) + dot(am, bm) + dot(al, bh)))


def _softplus(x):
    return jnp.maximum(x, 0.0) + jnp.log1p(jnp.exp(-jnp.abs(x)))


def _sigmoid(x):
    return 1.0 / (1.0 + jnp.exp(-x))


def _params(sem):
    return pltpu.CompilerParams(dimension_semantics=sem, vmem_limit_bytes=VMEM_LIMIT)


def _inproj_kernel(x_ref, w_ref, o_r, o_l, o_s):
    xb = x_ref[...].astype(BF16)
    col = 0
    for o_ref in (o_r, o_l, o_s):
        width = o_ref.shape[1]
        for c0 in range(0, width, 384):
            c1 = min(c0 + 384, width)
            o_ref[:, c0:c1] = _dot(xb, w_ref[:, col + c0:col + c1])
        col += width


def _inproj(x2, w_bf16, tm=512):
    t, d = x2.shape
    n = w_bf16.shape[1]
    return pl.pallas_call(
        _inproj_kernel,
        grid=(t // tm,),
        in_specs=[pl.BlockSpec((tm, d), lambda i: (i, 0)),
                  pl.BlockSpec((d, n), lambda i: (0, 0))],
        out_specs=[pl.BlockSpec((tm, RWKV_COLS), lambda i: (i, 0)),
                   pl.BlockSpec((tm, LRU_COLS), lambda i: (i, 0)),
                   pl.BlockSpec((tm, SBA_COLS), lambda i: (i, 0))],
        out_shape=[jax.ShapeDtypeStruct((t, RWKV_COLS), F32),
                   jax.ShapeDtypeStruct((t, LRU_COLS), F32),
                   jax.ShapeDtypeStruct((t, SBA_COLS), F32)],
        compiler_params=_params(("parallel",)),
        name="inproj",
    )(x2, w_bf16)


def _rwkv_kernel(p_ref, mu_ref, w0_ref, a0_ref, kk_ref, ka_ref, rk_ref, lng_ref, lnb_ref,
                 wup_ref, aup_ref, gup_ref, ones_ref, o_ref, carry_ref, ht_ref):
    c = RWKV_CHUNK
    w = RWKV_WIDTH
    t = pl.program_id(1)

    @pl.when(t == 0)
    def _():
        carry_ref[...] = jnp.zeros_like(carry_ref)
        ht_ref[...] = jnp.zeros_like(ht_ref)

    p = p_ref[0]
    row = lax.broadcasted_iota(jnp.int32, (c, 1), 0)
    prev = jnp.where(row == 0, carry_ref[0:1, :], pltpu.roll(p, 1, axis=0))
    carry_ref[0:1, :] = p_ref[0, c - 1:c, :]
    ps = p + mu_ref[...] * (prev - p)

    r = ps[:, 0:w]
    k = ps[:, w:2 * w]
    v = ps[:, 2 * w:3 * w]
    lora_in = ps[:, 3 * w:3 * w + LANES]
    dg = ps[:, 3 * w + LANES:3 * w + 2 * LANES]

    w_log = -_softplus(-(w0_ref[...] + _dot3(jnp.tanh(lora_in), wup_ref[...]))) - 0.5
    logw = -jnp.exp(w_log)
    a = _sigmoid(a0_ref[...] + _dot3(lora_in, aup_ref[...]))
    g = _dot(_sigmoid(dg).astype(BF16), gup_ref[...])

    seg = ones_ref[...]
    kkr = k * kk_ref[...]
    kk = kkr / jnp.maximum(jnp.sqrt(_dot_sel_rhs(kkr * kkr, seg)), 1e-12)
    k2 = k * (1.0 + (a - 1.0) * ka_ref[...])
    b = kk * a

    ri = lax.broadcasted_iota(jnp.int32, (c, c), 0)
    ci = lax.broadcasted_iota(jnp.int32, (c, c), 1)
    tri = jnp.where(ci <= ri, 1.0, 0.0).astype(BF16)
    cum = _dot_sel_lhs(tri, logw)
    cum_tot = cum[c - 1:c, :]
    rt = r * jnp.exp(cum)
    at = -kk * jnp.exp(cum - logw)
    e_neg = jnp.exp(-cum)
    bt = b * e_neg
    kt = k2 * e_neg
    e_rem = jnp.exp(cum_tot - cum)
    bh = b * e_rem
    kh = k2 * e_rem
    e_tot = jnp.exp(cum_tot)

    lane = lax.broadcasted_iota(jnp.int32, (1, LANES), 1)
    first = lane < HEAD_DIM

    def stack(x):
        return jnp.concatenate([jnp.where(first, x, 0.0), jnp.where(first, 0.0, x)], axis=0)

    ri2 = lax.broadcasted_iota(jnp.int32, (2 * c, 2 * c), 0)
    ci2 = lax.broadcasted_iota(jnp.int32, (2 * c, 2 * c), 1)
    same = (ri2 // c) == (ci2 // c)
    strict = same & (ci2 < ri2)
    incl = same & (ci2 <= ri2)
    eye = jnp.where(ri2 == ci2, 1.0, 0.0)

    outs = []
    for j in range(w // LANES):
        sl = slice(j * LANES, (j + 1) * LANES)
        at_d, rt_d, bt_d, kt_d = stack(at[:, sl]), stack(rt[:, sl]), stack(bt[:, sl]), stack(kt[:, sl])
        v_d, bh_d, kh_d = stack(v[:, sl]), stack(bh[:, sl]), stack(kh[:, sl])
        ar = jnp.concatenate([at_d, rt_d], axis=0).astype(BF16)
        bk = jnp.concatenate([bt_d, kt_d], axis=0).astype(BF16)
        gram = _dot_nt(ar, bk)
        a_ab = jnp.where(strict, gram[0:2 * c, 0:2 * c], 0.0)
        a_ak = jnp.where(strict, gram[0:2 * c, 2 * c:4 * c], 0.0)
        a_rb = jnp.where(incl, gram[2 * c:4 * c, 0:2 * c], 0.0)
        a_rk = jnp.where(incl, gram[2 * c:4 * c, 2 * c:4 * c], 0.0)

        pw = a_ab
        inv = eye + a_ab
        for _ in range(5):
            pw = _dot3(pw, pw)
            inv = inv + _dot3(inv, pw)

        ht = ht_ref[j]
        xh = _dot_nt(ar, ht.astype(BF16))
        v_b = v_d.astype(BF16)
        rhs_u = xh[0:2 * c] + _dot(a_ak.astype(BF16), v_b)
        u_d = _dot3(inv, rhs_u)
        uv = jnp.concatenate([u_d.astype(BF16), v_b], axis=0)
        o_d = xh[2 * c:4 * c] + _dot(jnp.concatenate([a_rb, a_rk], axis=1).astype(BF16), uv)
        outs.append(o_d[0:c] + o_d[c:2 * c])
        bkh = jnp.concatenate([bh_d, kh_d], axis=0).astype(BF16)
        ht_ref[j] = ht * e_tot[:, sl] + _dot_tn(uv, bkh)

    o = jnp.concatenate(outs, axis=1)
    inv_n = 1.0 / HEAD_DIM
    mean = _dot_sel_rhs(o, seg) * inv_n
    d = o - mean
    var = _dot_sel_rhs(d * d, seg) * inv_n
    on = d * lax.rsqrt(var + RWKV_GN_EPS) * lng_ref[...] + lnb_ref[...]
    bonus = _dot_sel_rhs(r * k2 * rk_ref[...], seg) * v
    o_ref[0] = (on + bonus) * g


def _rwkv(p_r, mu, w0, wup, a0, aup, gup, k_k, k_a, r_k, ln_g, ln_b):
    bsz, seq, _ = p_r.shape
    c = RWKV_CHUNK
    w = RWKV_WIDTH
    zeros = jnp.zeros((DECAY_RANK, w), F32)
    wup_pad = jnp.concatenate([wup, zeros], axis=0)
    aup_pad = jnp.concatenate([zeros, aup], axis=0)
    head = jnp.arange(w) // HEAD_DIM
    ones = (head[:, None] == head[None, :]).astype(BF16)
    vec = lambda x: x.reshape(1, -1)
    full = lambda shape: pl.BlockSpec(shape, lambda b, t: (0,) * len(shape))
    return pl.pallas_call(
        _rwkv_kernel,
        grid=(bsz, seq // c),
        in_specs=[pl.BlockSpec((1, c, RWKV_COLS), lambda b, t: (b, t, 0)),
                  full((1, RWKV_COLS))] + [full((1, w))] * 7
                 + [full((LANES, w)), full((LANES, w)), full((GATE_RANK, w)), full((w, w))],
        out_specs=pl.BlockSpec((1, c, w), lambda b, t: (b, t, 0)),
        out_shape=jax.ShapeDtypeStruct((bsz, seq, w), F32),
        scratch_shapes=[pltpu.VMEM((8, RWKV_COLS), F32),
                        pltpu.VMEM((w // LANES, LANES, LANES), F32)],
        compiler_params=_params(("parallel", "arbitrary")),
        name="rwkv7",
    )(p_r, vec(mu), vec(w0), vec(a0), vec(k_k), vec(k_a), vec(r_k), vec(ln_g), vec(ln_b),
      wup_pad, aup_pad, gup.astype(BF16), ones)


def _lru_kernel(p_ref, cw_ref, cb_ref, wax_ref, bax_ref, lam_ref, ng_ref, ones_ref, o_ref,
                xs_ref, h_ref):
    tb = p_ref.shape[1]
    w = LRU_WIDTH
    hist = 8
    t = pl.program_id(1)

    @pl.when(t == 0)
    def _():
        xs_ref[0:hist, :] = jnp.zeros((hist, w), F32)
        h_ref[...] = jnp.zeros_like(h_ref)

    @pl.when(t > 0)
    def _():
        xs_ref[0:hist, :] = xs_ref[tb:tb + hist, :]

    gate_in = p_ref[0, :, 0:w]
    xin = p_ref[0, :, w:2 * w]
    xs_ref[hist:hist + tb, :] = xin
    xc = cb_ref[...] + cw_ref[CONV_WIDTH - 1:CONV_WIDTH, :] * xin
    for j in range(1, CONV_WIDTH):
        xc = xc + cw_ref[CONV_WIDTH - 1 - j:CONV_WIDTH - j, :] * xs_ref[hist - j:hist - j + tb, :]

    gates = _dot(xc.astype(BF16), wax_ref[...]) + bax_ref[...]
    gate_a = _sigmoid(gates[:, 0:w])
    gate_x = _sigmoid(gates[:, w:2 * w])
    log_a = -LRU_C * gate_a * _softplus(-lam_ref[...])
    a = jnp.exp(log_a)
    bb = jnp.sqrt(1.0 - jnp.exp(2.0 * log_a)) * (gate_x * xc)

    row = lax.broadcasted_iota(jnp.int32, (tb, 1), 0)
    d = 1
    while d < tb:
        keep = row >= d
        a_s = jnp.where(keep, pltpu.roll(a, d, axis=0), 1.0)
        b_s = jnp.where(keep, pltpu.roll(bb, d, axis=0), 0.0)
        bb = a * b_s + bb
        a = a * a_s
        d *= 2
    h = bb + a * h_ref[0:1, :]
    h_ref[0:1, :] = h[tb - 1:tb, :]

    gelu = 0.5 * gate_in * (1.0 + jnp.tanh(0.7978845608028654 * (gate_in + 0.044715 * (gate_in * gate_in * gate_in))))
    y = gelu * h
    ms = _dot_sel_rhs(y * y, ones_ref[...]) * (1.0 / HEAD_DIM)
    o_ref[0] = y * lax.rsqrt(ms + RMS_EPS) * ng_ref[...]


def _block_diag(wb):
    n, d, _ = wb.shape
    eye = jnp.eye(n, dtype=wb.dtype)
    return jnp.einsum('gij,gh->gihj', wb, eye).reshape(n * d, n * d)


def _lru(p_l, conv_w, conv_b, wa, ba, wx, bx, lam, norm_g, tb=512):
    bsz, seq, _ = p_l.shape
    w = LRU_WIDTH
    tb = min(tb, seq)
    wax = jnp.concatenate([_block_diag(wa), _block_diag(wx)], axis=1).astype(BF16)
    bax = jnp.concatenate([ba, bx]).reshape(1, 2 * w)
    head = jnp.arange(w) // HEAD_DIM
    ones = (head[:, None] == head[None, :]).astype(BF16)
    full = lambda shape: pl.BlockSpec(shape, lambda b, t: (0,) * len(shape))
    return pl.pallas_call(
        _lru_kernel,
        grid=(bsz, seq // tb),
        in_specs=[pl.BlockSpec((1, tb, LRU_COLS), lambda b, t: (b, t, 0)),
                  full((CONV_WIDTH, w)), full((1, w)), full((w, 2 * w)), full((1, 2 * w)),
                  full((1, w)), full((1, w)), full((w, w))],
        out_specs=pl.BlockSpec((1, tb, w), lambda b, t: (b, t, 0)),
        out_shape=jax.ShapeDtypeStruct((bsz, seq, w), F32),
        scratch_shapes=[pltpu.VMEM((tb + 8, w), F32), pltpu.VMEM((8, w), F32)],
        compiler_params=_params(("parallel", "arbitrary")),
        name="rglru",
    )(p_l, conv_w, conv_b.reshape(1, w), wax, bax, lam.reshape(1, w), norm_g.reshape(1, w), ones)


def _sba_kernel(q_ref, k_ref, v_ref, g_ref, cs_ref, ones_ref, o_ref, *, blk):
    qi = pl.program_id(2)
    scale = HEAD_DIM ** -0.5
    lane = lax.broadcasted_iota(jnp.int32, (1, LANES), 1)
    first = lane < HEAD_DIM
    q = q_ref[0]
    qh = (jnp.where(first, q, 0.0).astype(BF16), jnp.where(first, 0.0, q).astype(BF16))
    cs = cs_ref[...]
    ti = lax.broadcasted_iota(jnp.int32, (blk, blk), 0)
    si = lax.broadcasted_iota(jnp.int32, (blk, blk), 1)
    causal = si < ti

    def tile(j, qb, carry, mask):
        kt = k_ref[0, pl.ds(j * blk, blk), :].astype(BF16)
        vt = v_ref[0, pl.ds(j * blk, blk), :].astype(BF16)
        z = _dot_nt(qb, kt) * scale
        ls = jnp.minimum(z, 0.0) - jnp.log1p(jnp.exp(-jnp.abs(z)))
        lk = ls - z
        if mask is not None:
            lk = jnp.where(mask, lk, 0.0)
        hi, mid, lo = _split3(lk)
        sums = _dot(hi, cs) + _dot(mid, cs) + _dot(lo, cs)
        attn = jnp.exp(ls + sums[:, 0:blk] + carry)
        if mask is not None:
            attn = jnp.where(mask, attn, 0.0)
        return _dot(attn.astype(BF16), vt), carry + sums[:, blk:2 * blk]

    zero = jnp.zeros((blk, blk), F32)
    acc0, c0 = tile(qi, qh[0], zero, causal)
    acc1, c1 = tile(qi, qh[1], zero, causal)

    def body(i, state):
        acc0, c0, acc1, c1 = state
        j = qi - 1 - i
        d0, c0 = tile(j, qh[0], c0, None)
        d1, c1 = tile(j, qh[1], c1, None)
        return acc0 + d0, c0, acc1 + d1, c1

    acc0, _, acc1, _ = lax.fori_loop(0, qi, body, (acc0, c0, acc1, c1))
    o = jnp.where(first, acc0, acc1)
    ms = _dot_sel_rhs(o * o, ones_ref[...]) * (1.0 / HEAD_DIM)
    o_ref[0] = o * lax.rsqrt(ms + RMS_EPS) * g_ref[...]


def _sba(p_s, norm_g, blk=128):
    bsz, seq, _ = p_s.shape
    pairs = SBA_WIDTH // LANES
    ji = jnp.arange(blk)
    cs = jnp.concatenate([(ji[:, None] > ji[None, :]), jnp.ones((blk, blk), bool)], axis=1).astype(BF16)
    head = jnp.arange(LANES) // HEAD_DIM
    ones = (head[:, None] == head[None, :]).astype(BF16)
    return pl.pallas_call(
        functools.partial(_sba_kernel, blk=blk),
        grid=(bsz, pairs, seq // blk),
        in_specs=[pl.BlockSpec((1, blk, LANES), lambda b, h, i: (b, i, h)),
                  pl.BlockSpec((1, seq, LANES), lambda b, h, i: (b, 0, pairs + h)),
                  pl.BlockSpec((1, seq, LANES), lambda b, h, i: (b, 0, 2 * pairs + h)),
                  pl.BlockSpec((1, LANES), lambda b, h, i: (0, h)),
                  pl.BlockSpec((blk, 2 * blk), lambda b, h, i: (0, 0)),
                  pl.BlockSpec((LANES, LANES), lambda b, h, i: (0, 0))],
        out_specs=pl.BlockSpec((1, blk, LANES), lambda b, h, i: (b, i, h)),
        out_shape=jax.ShapeDtypeStruct((bsz, seq, SBA_WIDTH), F32),
        compiler_params=_params(("parallel", "parallel", "arbitrary")),
        name="stickbreak",
    )(p_s, p_s, p_s, norm_g.reshape(1, SBA_WIDTH), cs, ones)


def _layer_norm(h, g, b):
    mu = jnp.mean(h, axis=-1, keepdims=True)
    d = h - mu
    var = jnp.mean(d * d, axis=-1, keepdims=True)
    return d * lax.rsqrt(var + LN_EPS) * g + b


def _outproj_kernel(yr_ref, yl_ref, ys_ref, x_ref, w_ref, g_ref, b_ref, rw_ref, rb_ref,
                    x1_ref, gt_ref):
    r0, r1 = RWKV_WIDTH, RWKV_WIDTH + LRU_WIDTH
    mix = (_dot(yr_ref[...].astype(BF16), w_ref[0:r0, :])
           + _dot(yl_ref[...].astype(BF16), w_ref[r0:r1, :])
           + _dot(ys_ref[...].astype(BF16), w_ref[r1:, :]))
    x1 = _layer_norm(ALPHA * x_ref[...] + mix, g_ref[...], b_ref[...])
    x1_ref[...] = x1

    logits = _dot3(rw_ref[...], x1, dot=_dot_nt) + rb_ref[...]
    mx = jnp.max(logits, axis=0, keepdims=True)
    ex = jnp.exp(logits - mx)
    scores = ex / jnp.sum(ex, axis=0, keepdims=True)
    s = [scores[e:e + 1, :] for e in range(N_EXPERTS)]
    top2 = []
    gsum = []
    for grp in range(N_GROUPS):
        mem = list(range(grp * EXPERTS_PER_GROUP, (grp + 1) * EXPERTS_PER_GROUP))
        tot = None
        for e in mem:
            rank = None
            for o in mem:
                if o == e:
                    continue
                ahead = (s[o] > s[e]) | ((s[o] == s[e]) if o < e else False)
                ahead = jnp.where(ahead, 1.0, 0.0)
                rank = ahead if rank is None else rank + ahead
            sel = rank < 1.5
            top2.append(sel)
            part = jnp.where(sel, s[e], 0.0)
            tot = part if tot is None else tot + part
        gsum.append(tot)
    best = gsum[0]
    best_g = jnp.zeros_like(best)
    for grp in range(1, N_GROUPS):
        upd = gsum[grp] > best
        best = jnp.where(upd, gsum[grp], best)
        best_g = jnp.where(upd, float(grp), best_g)
    rows = []
    for e in range(N_EXPERTS):
        grp = e // EXPERTS_PER_GROUP
        rows.append(jnp.where((best_g == float(grp)) & top2[e], s[e] / gsum[grp], 0.0))
    gt_ref[...] = jnp.concatenate(rows, axis=0)


def _outproj(y_r, y_l, y_s, x2, w_bf16, ln_g, ln_b, router_wt, router_b, tm=512):
    t, d = x2.shape
    row = lambda width: pl.BlockSpec((tm, width), lambda i: (i, 0))
    full = lambda shape: pl.BlockSpec(shape, lambda i: (0,) * len(shape))
    return pl.pallas_call(
        _outproj_kernel,
        grid=(t // tm,),
        in_specs=[row(RWKV_WIDTH), row(LRU_WIDTH), row(SBA_WIDTH), row(d), full((d, d)),
                  full((1, d)), full((1, d)), full((N_EXPERTS, d)), full((N_EXPERTS, 1))],
        out_specs=[row(d), pl.BlockSpec((N_EXPERTS, tm), lambda i: (0, i))],
        out_shape=[jax.ShapeDtypeStruct((t, d), F32), jax.ShapeDtypeStruct((N_EXPERTS, t), F32)],
        compiler_params=_params(("parallel",)),
        name="outproj_ln_router",
    )(y_r, y_l, y_s, x2, w_bf16, ln_g.reshape(1, d), ln_b.reshape(1, d), router_wt,
      router_b.reshape(N_EXPERTS, 1))


def _moe_kernel(x_ref, gate_ref, wg_ref, wu_ref, wd_ref, g_ref, b_ref, o_ref, xb_ref, acc_ref):
    e = pl.program_id(1)

    @pl.when(e == 0)
    def _():
        xb_ref[...] = x_ref[...].astype(BF16)
        acc_ref[...] = jnp.zeros_like(acc_ref)

    lane = lax.broadcasted_iota(jnp.int32, (1, N_EXPERTS), 1)
    gcol = jnp.sum(jnp.where(lane == e, gate_ref[...], 0.0), axis=1, keepdims=True)
    xb = xb_ref[...]
    hg = _dot(xb, wg_ref[0])
    hu = _dot(xb, wu_ref[0])
    h = hg * _sigmoid(hg) * hu * gcol
    acc_ref[...] += _dot(h.astype(BF16), wd_ref[0])

    @pl.when(e == N_EXPERTS - 1)
    def _():
        o_ref[...] = _layer_norm(ALPHA * x_ref[...] + acc_ref[...], g_ref[...], b_ref[...])


def _moe(x1, gates, wg, wu, wd, ln_g, ln_b, tm=1024):
    t, d = x1.shape
    de = wg.shape[2]
    tm = min(tm, t)
    return pl.pallas_call(
        _moe_kernel,
        grid=(t // tm, N_EXPERTS),
        in_specs=[pl.BlockSpec((tm, d), lambda i, e: (i, 0)),
                  pl.BlockSpec((tm, N_EXPERTS), lambda i, e: (i, 0)),
                  pl.BlockSpec((1, d, de), lambda i, e: (e, 0, 0)),
                  pl.BlockSpec((1, d, de), lambda i, e: (e, 0, 0)),
                  pl.BlockSpec((1, de, d), lambda i, e: (e, 0, 0)),
                  pl.BlockSpec((1, d), lambda i, e: (0, 0)),
                  pl.BlockSpec((1, d), lambda i, e: (0, 0))],
        out_specs=pl.BlockSpec((tm, d), lambda i, e: (i, 0)),
        out_shape=jax.ShapeDtypeStruct((t, d), F32),
        scratch_shapes=[pltpu.VMEM((tm, d), BF16), pltpu.VMEM((tm, d), F32)],
        compiler_params=_params(("parallel", "arbitrary")),
        name="moe_ln",
    )(x1, gates, wg, wu, wd, ln_g.reshape(1, d), ln_b.reshape(1, d))


def kernel(x, w_in, shift_mu, rwkv_w0, rwkv_w_up, rwkv_a0, rwkv_a_up, rwkv_g_up, rwkv_k_k, rwkv_k_a, rwkv_r_k, rwkv_ln_g, rwkv_ln_b, lru_conv_w, lru_conv_b, lru_wa, lru_ba, lru_wx, lru_bx, lru_lambda, lru_norm_g, sba_norm_g, w_out, ln1_g, ln1_b, ln2_g, ln2_b, router_w, router_b, exp_w_gate, exp_w_up, exp_w_down):
    bsz, seq, d = x.shape
    t = bsz * seq
    x2 = x.reshape(t, d)
    router_wt = router_w.T
    for l in range(w_in.shape[0]):
        p_r, p_l, p_s = _inproj(x2, w_in[l].astype(BF16))
        y_r = _rwkv(p_r.reshape(bsz, seq, RWKV_COLS), shift_mu[l], rwkv_w0[l], rwkv_w_up[l],
                    rwkv_a0[l], rwkv_a_up[l], rwkv_g_up[l], rwkv_k_k[l], rwkv_k_a[l],
                    rwkv_r_k[l], rwkv_ln_g[l], rwkv_ln_b[l])
        y_l = _lru(p_l.reshape(bsz, seq, LRU_COLS), lru_conv_w[l], lru_conv_b[l], lru_wa[l],
                   lru_ba[l], lru_wx[l], lru_bx[l], lru_lambda[l], lru_norm_g[l])
        y_s = _sba(p_s.reshape(bsz, seq, SBA_COLS), sba_norm_g[l])
        x1, gates_t = _outproj(y_r.reshape(t, RWKV_WIDTH), y_l.reshape(t, LRU_WIDTH),
                               y_s.reshape(t, SBA_WIDTH), x2, w_out[l].astype(BF16),
                               ln1_g[l], ln1_b[l], router_wt, router_b)
        x2 = _moe(x1, gates_t.T, exp_w_gate[l].astype(BF16), exp_w_up[l].astype(BF16),
                  exp_w_down[l].astype(BF16), ln2_g[l], ln2_b[l])
    return x2.reshape(bsz, seq, d)
```

```python
import functools

import jax
import jax.numpy as jnp
from jax import lax
from jax.experimental import pallas as pl
from jax.experimental.pallas import tpu as pltpu

F32 = jnp.float32
BF16 = jnp.bfloat16

HEAD_DIM = 64
RWKV_WIDTH = 384
LRU_WIDTH = 256
SBA_WIDTH = 384
DECAY_RANK = 64
ICLR_RANK = 64
GATE_RANK = 128
RWKV_COLS = 3 * RWKV_WIDTH + DECAY_RANK + ICLR_RANK + GATE_RANK
LRU_COLS = 2 * LRU_WIDTH
SBA_COLS = 3 * SBA_WIDTH
RWKV_GN_EPS = 64e-5
CONV_WIDTH = 4
LRU_C = 8.0
N_EXPERTS = 16
N_GROUPS = 4
EXPERTS_PER_GROUP = 4
DEPTH = 4
ALPHA = (2.0 * DEPTH) ** 0.25
LN_EPS = 1e-5
RMS_EPS = 1e-6
LOG2E = 1.4426950408889634

LANES = 128
RWKV_CHUNK = 64
SBA_KEY_TILE = 256
VMEM_LIMIT = 48 * 1024 * 1024


def _dot(a, b):
    return jnp.dot(a, b, preferred_element_type=F32)


def _dot_nt(a, b):
    return lax.dot_general(a, b, (((1,), (1,)), ((), ())), preferred_element_type=F32)


def _dot_tn(a, b):
    return lax.dot_general(a, b, (((0,), (0,)), ((), ())), preferred_element_type=F32)


def _split3(x):
    hi = x.astype(BF16)
    r1 = x - hi.astype(F32)
    mid = r1.astype(BF16)
    lo = (r1 - mid.astype(F32)).astype(BF16)
    return hi, mid, lo


def _dot_sel_rhs(x, sel):
    hi, mid, lo = _split3(x)
    return _dot(hi, sel) + _dot(mid, sel) + _dot(lo, sel)


def _dot_sel_lhs(sel, x):
    hi, mid, lo = _split3(x)
    return _dot(sel, hi) + _dot(sel, mid) + _dot(sel, lo)


def _dot3(a, b, dot=_dot):
    ah, am, al = _split3(a)
    bh, bm, bl = _split3(b)
    return (dot(ah, bh) + (dot(ah, bm) + dot(am, bh))
            + (dot(ah, bl) + dot(am, bm) + dot(al, bh)))


def _softplus(x):
    return jnp.maximum(x, 0.0) + jnp.log1p(jnp.exp(-jnp.abs(x)))


def _sigmoid(x):
    return 1.0 / (1.0 + jnp.exp(-x))


def _params(sem):
    return pltpu.CompilerParams(dimension_semantics=sem, vmem_limit_bytes=VMEM_LIMIT)


def _inproj_kernel(x_ref, w_ref, wkt_ref, o_r, o_l, o_q, o_kt, o_v):
    xb = x_ref[...].astype(BF16)
    col = 0
    for o_ref, skip in ((o_r, 0), (o_l, 0), (o_q, SBA_WIDTH), (o_v, 0)):
        width = o_ref.shape[1]
        for c0 in range(0, width, 384):
            c1 = min(c0 + 384, width)
            o_ref[:, c0:c1] = _dot(xb, w_ref[:, col + c0:col + c1]).astype(o_ref.dtype)
        col += width + skip
    kt = _dot_nt(wkt_ref[...], xb)
    tk = o_kt.shape[2]
    for i in range(o_kt.shape[0]):
        o_kt[i] = kt[:, i * tk:(i + 1) * tk].astype(o_kt.dtype)


def _inproj(x2, w_bf16, tm=512, tk=SBA_KEY_TILE):
    t, d = x2.shape
    n = w_bf16.shape[1]
    k0 = RWKV_COLS + LRU_COLS + SBA_WIDTH
    wkt = w_bf16[:, k0:k0 + SBA_WIDTH].T
    return pl.pallas_call(
        _inproj_kernel,
        grid=(t // tm,),
        in_specs=[pl.BlockSpec((tm, d), lambda i: (i, 0)),
                  pl.BlockSpec((d, n), lambda i: (0, 0)),
                  pl.BlockSpec((SBA_WIDTH, d), lambda i: (0, 0))],
        out_specs=[pl.BlockSpec((tm, RWKV_COLS), lambda i: (i, 0)),
                   pl.BlockSpec((tm, LRU_COLS), lambda i: (i, 0)),
                   pl.BlockSpec((tm, SBA_WIDTH), lambda i: (i, 0)),
                   pl.BlockSpec((tm // tk, SBA_WIDTH, tk), lambda i: (i, 0, 0)),
                   pl.BlockSpec((tm, SBA_WIDTH), lambda i: (i, 0))],
        out_shape=[jax.ShapeDtypeStruct((t, RWKV_COLS), F32),
                   jax.ShapeDtypeStruct((t, LRU_COLS), F32),
                   jax.ShapeDtypeStruct((t, SBA_WIDTH), BF16),
                   jax.ShapeDtypeStruct((t // tk, SBA_WIDTH, tk), BF16),
                   jax.ShapeDtypeStruct((t, SBA_WIDTH), BF16)],
        compiler_params=_params(("parallel",)),
        name="inproj",
    )(x2, w_bf16, wkt)


def _rwkv_kernel(p_ref, mu_ref, w0_ref, a0_ref, kk_ref, ka_ref, rk_ref, lng_ref, lnb_ref,
                 wup_ref, aup_ref, gup_ref, ones_ref, o_ref, carry_ref, ht_ref, *, nb):
    c = RWKV_CHUNK
    w = RWKV_WIDTH
    rows = nb * c
    t = pl.program_id(1)

    @pl.when(t == 0)
    def _():
        carry_ref[...] = jnp.zeros_like(carry_ref)
        ht_ref[...] = jnp.zeros_like(ht_ref)

    p = p_ref[...].reshape(rows, RWKV_COLS)
    row = lax.broadcasted_iota(jnp.int32, (rows, 1), 0)
    prev = pltpu.roll(p, 1, axis=0)
    for n in range(nb):
        prev = jnp.where(row == n * c, carry_ref[n:n + 1, :], prev)
    for n in range(nb):
        carry_ref[n:n + 1, :] = p_ref[n, c - 1:c, :]
    ps = p + mu_ref[...] * (prev - p)

    r = ps[:, 0:w]
    k = ps[:, w:2 * w]
    v = ps[:, 2 * w:3 * w]
    lora_in = ps[:, 3 * w:3 * w + LANES]
    dg = ps[:, 3 * w + LANES:3 * w + 2 * LANES]

    w_log = -_softplus(-(w0_ref[...] + _dot(jnp.tanh(lora_in).astype(BF16), wup_ref[...]))) - 0.5
    logw = -jnp.exp(w_log)
    a = _sigmoid(a0_ref[...] + _dot(lora_in.astype(BF16), aup_ref[...]))
    g = _dot(_sigmoid(dg).astype(BF16), gup_ref[...])

    ones = ones_ref[...]

    def seg_sum(x):
        return jnp.concatenate([_dot(x[:, i:i + LANES].astype(BF16), ones) for i in range(0, w, LANES)], axis=1)

    kkr = k * kk_ref[...]
    kk = kkr / jnp.maximum(jnp.sqrt(seg_sum(kkr * kkr)), 1e-12)
    k2 = k * (1.0 + (a - 1.0) * ka_ref[...])
    b = kk * a

    ri = lax.broadcasted_iota(jnp.int32, (rows, rows), 0)
    ci = lax.broadcasted_iota(jnp.int32, (rows, rows), 1)
    tri = jnp.where(((ri // c) == (ci // c)) & (ci <= ri), 1.0, 0.0).astype(BF16)
    lw_hi = logw.astype(BF16)
    lw_lo = (logw - lw_hi.astype(F32)).astype(BF16)
    cum = _dot(tri, lw_hi) + _dot(tri, lw_lo)

    lane = lax.broadcasted_iota(jnp.int32, (1, LANES), 1)
    first = lane < HEAD_DIM

    def stack(x):
        return jnp.concatenate([jnp.where(first, x, 0.0), jnp.where(first, 0.0, x)], axis=0)

    ri2 = lax.broadcasted_iota(jnp.int32, (2 * c, 2 * c), 0)
    ci2 = lax.broadcasted_iota(jnp.int32, (2 * c, 2 * c), 1)
    same = (ri2 // c) == (ci2 // c)
    strict = same & (ci2 < ri2)
    incl = same & (ci2 <= ri2)
    eye = jnp.where(ri2 == ci2, 1.0, 0.0)

    def dotb(x, y):
        return _dot(x.astype(BF16), y.astype(BF16))

    cum_tot = jnp.concatenate(
        [jnp.broadcast_to(cum[n * c + c - 1:n * c + c, :], (c, w)) for n in range(nb)], axis=0)
    rt = r * jnp.exp(cum)
    at = -kk * jnp.exp(cum - logw)
    e_neg = jnp.exp(-cum)
    bt = b * e_neg
    kt = k2 * e_neg
    e_rem = jnp.exp(cum_tot - cum)
    bh = b * e_rem
    kh = k2 * e_rem
    e_tot = jnp.exp(cum_tot)

    chains = [(n, j) for n in range(nb) for j in range(w // LANES)]

    def cut(x, n, j):
        return x[n * c:(n + 1) * c, j * LANES:(j + 1) * LANES]

    ar = [jnp.concatenate([stack(cut(at, n, j)), stack(cut(rt, n, j))], axis=0).astype(BF16)
          for n, j in chains]
    bk = [jnp.concatenate([stack(cut(bt, n, j)), stack(cut(kt, n, j))], axis=0).astype(BF16)
          for n, j in chains]
    gram = [_dot_nt(x, y) for x, y in zip(ar, bk)]
    ht = [ht_ref[n, j] for n, j in chains]
    xh = [_dot_nt(x, h.astype(BF16)) for x, h in zip(ar, ht)]
    v_b = [stack(cut(v, n, j)).astype(BF16) for n, j in chains]
    a_ab = [jnp.where(strict, gm[0:2 * c, 0:2 * c], 0.0) for gm in gram]
    a_ak = [jnp.where(strict, gm[0:2 * c, 2 * c:4 * c], 0.0).astype(BF16) for gm in gram]
    a_r = [jnp.concatenate([jnp.where(incl, gm[2 * c:4 * c, 0:2 * c], 0.0),
                            jnp.where(incl, gm[2 * c:4 * c, 2 * c:4 * c], 0.0)], axis=1).astype(BF16)
           for gm in gram]
    rhs_u = [x[0:2 * c] + _dot(ak, vb) for x, ak, vb in zip(xh, a_ak, v_b)]

    pw = [x.astype(BF16) for x in a_ab]
    inv = [eye + x for x in a_ab]
    for _ in range(5):
        pw_f = [_dot(x, x) for x in pw]
        pw = [x.astype(BF16) for x in pw_f]
        inv = [x + _dot(x.astype(BF16), y) for x, y in zip(inv, pw)]

    u_d = [dotb(x, y) for x, y in zip(inv, rhs_u)]
    uv = [jnp.concatenate([u.astype(BF16), vb], axis=0) for u, vb in zip(u_d, v_b)]
    o_d = [x[2 * c:4 * c] + _dot(a, y) for x, a, y in zip(xh, a_r, uv)]
    bkh = [jnp.concatenate([stack(cut(bh, n, j)), stack(cut(kh, n, j))], axis=0).astype(BF16)
           for n, j in chains]
    upd = [_dot_tn(y, z) for y, z in zip(uv, bkh)]
    for (n, j), h, u in zip(chains, ht, upd):
        ht_ref[n, j] = h * e_tot[n * c:n * c + 1, j * LANES:(j + 1) * LANES] + u
    o_pair = [x[0:c] + x[c:2 * c] for x in o_d]
    npairs = w // LANES
    o = jnp.concatenate([jnp.concatenate(o_pair[n * npairs:(n + 1) * npairs], axis=1)
                         for n in range(nb)], axis=0)
    inv_n = 1.0 / HEAD_DIM
    d = o - seg_sum(o) * inv_n
    var = seg_sum(d * d) * inv_n
    on = d * lax.rsqrt(var + RWKV_GN_EPS) * lng_ref[...] + lnb_ref[...]
    bonus = seg_sum(r * k2 * rk_ref[...]) * v
    o_ref[...] = ((on + bonus) * g).reshape(nb, c, w)


def _rwkv(p_r, mu, w0, wup, a0, aup, gup, k_k, k_a, r_k, ln_g, ln_b, nb=4):
    bsz, seq, _ = p_r.shape
    nb = min(nb, bsz)
    c = RWKV_CHUNK
    w = RWKV_WIDTH
    zeros = jnp.zeros((DECAY_RANK, w), F32)
    wup_pad = jnp.concatenate([wup, zeros], axis=0).astype(BF16)
    aup_pad = jnp.concatenate([zeros, aup], axis=0).astype(BF16)
    head = jnp.arange(LANES) // HEAD_DIM
    ones = (head[:, None] == head[None, :]).astype(BF16)
    vec = lambda x: x.reshape(1, -1)
    full = lambda shape: pl.BlockSpec(shape, lambda b, t: (0,) * len(shape))
    return pl.pallas_call(
        functools.partial(_rwkv_kernel, nb=nb),
        grid=(bsz // nb, seq // c),
        in_specs=[pl.BlockSpec((nb, c, RWKV_COLS), lambda b, t: (b, t, 0)),
                  full((1, RWKV_COLS))] + [full((1, w))] * 7
                 + [full((LANES, w)), full((LANES, w)), full((GATE_RANK, w)), full((LANES, LANES))],
        out_specs=pl.BlockSpec((nb, c, w), lambda b, t: (b, t, 0)),
        out_shape=jax.ShapeDtypeStruct((bsz, seq, w), F32),
        scratch_shapes=[pltpu.VMEM((8, RWKV_COLS), F32),
                        pltpu.VMEM((nb, w // LANES, LANES, LANES), F32)],
        compiler_params=_params(("parallel", "arbitrary")),
        name="rwkv7",
    )(p_r, vec(mu), vec(w0), vec(a0), vec(k_k), vec(k_a), vec(r_k), vec(ln_g), vec(ln_b),
      wup_pad, aup_pad, gup.astype(BF16), ones)


def _lru_kernel(p_ref, cw_ref, cb_ref, wax_ref, bax_ref, lam_ref, ng_ref, ones_ref, o_ref,
                xs_ref, h_ref):
    tb = p_ref.shape[1]
    w = LRU_WIDTH
    hist = 8
    t = pl.program_id(1)

    @pl.when(t == 0)
    def _():
        xs_ref[0:hist, :] = jnp.zeros((hist, w), F32)
        h_ref[...] = jnp.zeros_like(h_ref)

    @pl.when(t > 0)
    def _():
        xs_ref[0:hist, :] = xs_ref[tb:tb + hist, :]

    gate_in = p_ref[0, :, 0:w]
    xin = p_ref[0, :, w:2 * w]
    xs_ref[hist:hist + tb, :] = xin
    xc = cb_ref[...] + cw_ref[CONV_WIDTH - 1:CONV_WIDTH, :] * xin
    for j in range(1, CONV_WIDTH):
        xc = xc + cw_ref[CONV_WIDTH - 1 - j:CONV_WIDTH - j, :] * xs_ref[hist - j:hist - j + tb, :]

    gates = _dot(xc.astype(BF16), wax_ref[...]) + bax_ref[...]
    gate_a = _sigmoid(gates[:, 0:w])
    gate_x = _sigmoid(gates[:, w:2 * w])
    log_a = -LRU_C * gate_a * _softplus(-lam_ref[...])
    a = jnp.exp(log_a)
    bb = jnp.sqrt(1.0 - jnp.exp(2.0 * log_a)) * (gate_x * xc)

    row = lax.broadcasted_iota(jnp.int32, (tb, 1), 0)
    d = 1
    while d < tb:
        keep = row >= d
        a_s = jnp.where(keep, pltpu.roll(a, d, axis=0), 1.0)
        b_s = jnp.where(keep, pltpu.roll(bb, d, axis=0), 0.0)
        bb = a * b_s + bb
        a = a * a_s
        d *= 2
    h = bb + a * h_ref[0:1, :]
    h_ref[0:1, :] = h[tb - 1:tb, :]

    gelu = 0.5 * gate_in * (1.0 + jnp.tanh(0.7978845608028654 * (gate_in + 0.044715 * (gate_in * gate_in * gate_in))))
    y = gelu * h
    ms = _dot((y * y).astype(BF16), ones_ref[...]) * (1.0 / HEAD_DIM)
    o_ref[0] = y * lax.rsqrt(ms + RMS_EPS) * ng_ref[...]


def _block_diag(wb):
    n, d, _ = wb.shape
    eye = jnp.eye(n, dtype=wb.dtype)
    return jnp.einsum('gij,gh->gihj', wb, eye).reshape(n * d, n * d)


def _lru(p_l, conv_w, conv_b, wa, ba, wx, bx, lam, norm_g, tb=512):
    bsz, seq, _ = p_l.shape
    w = LRU_WIDTH
    tb = min(tb, seq)
    wax = jnp.concatenate([_block_diag(wa), _block_diag(wx)], axis=1).astype(BF16)
    bax = jnp.concatenate([ba, bx]).reshape(1, 2 * w)
    head = jnp.arange(w) // HEAD_DIM
    ones = (head[:, None] == head[None, :]).astype(BF16)
    full = lambda shape: pl.BlockSpec(shape, lambda b, t: (0,) * len(shape))
    return pl.pallas_call(
        _lru_kernel,
        grid=(bsz, seq // tb),
        in_specs=[pl.BlockSpec((1, tb, LRU_COLS), lambda b, t: (b, t, 0)),
                  full((CONV_WIDTH, w)), full((1, w)), full((w, 2 * w)), full((1, 2 * w)),
                  full((1, w)), full((1, w)), full((w, w))],
        out_specs=pl.BlockSpec((1, tb, w), lambda b, t: (b, t, 0)),
        out_shape=jax.ShapeDtypeStruct((bsz, seq, w), F32),
        scratch_shapes=[pltpu.VMEM((tb + 8, w), F32), pltpu.VMEM((8, w), F32)],
        compiler_params=_params(("parallel", "arbitrary")),
        name="rglru",
    )(p_l, conv_w, conv_b.reshape(1, w), wax, bax, lam.reshape(1, w), norm_g.reshape(1, w), ones)


def _sba_kernel(q_ref, kt_ref, v_ref, g_ref, cs_ref, ones_ref, o_ref, acc_ref, car_ref, *, tq, tk):
    qi = pl.program_id(2)
    nsub = tq // tk
    lane = lax.broadcasted_iota(jnp.int32, (1, LANES), 1)
    first = lane < HEAD_DIM
    q = q_ref[0] * jnp.asarray(HEAD_DIM ** -0.5, BF16)
    zero = jnp.zeros_like(q)
    qh = (jnp.where(first, q, zero), jnp.where(first, zero, q))
    cs = cs_ref[...]
    acc_ref[...] = jnp.zeros_like(acc_ref)
    car_ref[...] = jnp.zeros_like(car_ref)

    def tile(j, r0, masked):
        kt = kt_ref[j]
        vt = v_ref[0, pl.ds(pl.multiple_of(j * tk, tk), tk), :]
        if masked:
            ti = lax.broadcasted_iota(jnp.int32, (tq - r0, tk), 0)
            si = lax.broadcasted_iota(jnp.int32, (tq - r0, tk), 1)
            mask = si < ti
        for h in range(2):
            z = _dot(qh[h][r0:], kt)
            nk = jnp.maximum(z, 0.0) + jnp.log(1.0 + jnp.exp2(jnp.abs(z) * -LOG2E))
            if masked:
                nk = jnp.where(mask, nk, 0.0)
            after = _dot(nk.astype(BF16), cs)
            car = car_ref[h, r0:, :]
            attn = jnp.exp((z - nk) + after + car)
            if masked:
                attn = jnp.where(mask, attn, 0.0)
            acc_ref[h, r0:, :] += _dot(attn.astype(BF16), vt)
            car_ref[h, r0:, :] = car + (after[:, 0:1] - nk[:, 0:1])

    for d in range(nsub - 1, -1, -1):
        tile(qi * nsub + d, d * tk, True)

    def body(i, _):
        tile(qi * nsub - 1 - i, 0, False)
        return 0

    lax.fori_loop(0, qi * nsub, body, 0)
    o = jnp.where(first, acc_ref[0], acc_ref[1])
    ms = _dot((o * o).astype(BF16), ones_ref[...]) * (1.0 / HEAD_DIM)
    o_ref[0] = o * lax.rsqrt(ms + RMS_EPS) * g_ref[...]


def _sba(q, kt, v, norm_g, tq=1024):
    bsz, seq, _ = q.shape
    tk = kt.shape[2]
    tq = min(tq, seq)
    pairs = SBA_WIDTH // LANES
    ji = jnp.arange(tk)
    cs = -(ji[:, None] > ji[None, :]).astype(BF16)
    head = jnp.arange(LANES) // HEAD_DIM
    ones = (head[:, None] == head[None, :]).astype(BF16)
    return pl.pallas_call(
        functools.partial(_sba_kernel, tq=tq, tk=tk),
        grid=(bsz, pairs, seq // tq),
        in_specs=[pl.BlockSpec((1, tq, LANES), lambda b, h, i: (b, i, h)),
                  pl.BlockSpec((seq // tk, LANES, tk), lambda b, h, i: (b, h, 0)),
                  pl.BlockSpec((1, seq, LANES), lambda b, h, i: (b, 0, h)),
                  pl.BlockSpec((1, LANES), lambda b, h, i: (0, h)),
                  pl.BlockSpec((tk, tk), lambda b, h, i: (0, 0)),
                  pl.BlockSpec((LANES, LANES), lambda b, h, i: (0, 0))],
        out_specs=pl.BlockSpec((1, tq, LANES), lambda b, h, i: (b, i, h)),
        out_shape=jax.ShapeDtypeStruct((bsz, seq, SBA_WIDTH), F32),
        scratch_shapes=[pltpu.VMEM((2, tq, LANES), F32), pltpu.VMEM((2, tq, 1), F32)],
        compiler_params=_params(("parallel", "parallel", "arbitrary")),
        name="stickbreak",
    )(q, kt, v, norm_g.reshape(1, SBA_WIDTH), cs, ones)


def _layer_norm(h, g, b):
    mu = jnp.mean(h, axis=-1, keepdims=True)
    d = h - mu
    var = jnp.mean(d * d, axis=-1, keepdims=True)
    return d * lax.rsqrt(var + LN_EPS) * g + b


def _outproj_kernel(yr_ref, yl_ref, ys_ref, x_ref, w_ref, g_ref, b_ref, rw_ref, rb_ref,
                    x1_ref, gt_ref):
    r0, r1 = RWKV_WIDTH, RWKV_WIDTH + LRU_WIDTH
    mix = (_dot(yr_ref[...].astype(BF16), w_ref[0:r0, :])
           + _dot(yl_ref[...].astype(BF16), w_ref[r0:r1, :])
           + _dot(ys_ref[...].astype(BF16), w_ref[r1:, :]))
    x1 = _layer_norm(ALPHA * x_ref[...] + mix, g_ref[...], b_ref[...])
    x1_ref[...] = x1

    logits = _dot3(rw_ref[...], x1, dot=_dot_nt) + rb_ref[...]
    mx = jnp.max(logits, axis=0, keepdims=True)
    ex = jnp.exp(logits - mx)
    scores = ex / jnp.sum(ex, axis=0, keepdims=True)
    s = [scores[e:e + 1, :] for e in range(N_EXPERTS)]
    top2 = []
    gsum = []
    for grp in range(N_GROUPS):
        mem = list(range(grp * EXPERTS_PER_GROUP, (grp + 1) * EXPERTS_PER_GROUP))
        tot = None
        for e in mem:
            rank = None
            for o in mem:
                if o == e:
                    continue
                ahead = (s[o] > s[e]) | ((s[o] == s[e]) if o < e else False)
                ahead = jnp.where(ahead, 1.0, 0.0)
                rank = ahead if rank is None else rank + ahead
            sel = rank < 1.5
            top2.append(sel)
            part = jnp.where(sel, s[e], 0.0)
            tot = part if tot is None else tot + part
        gsum.append(tot)
    best = gsum[0]
    best_g = jnp.zeros_like(best)
    for grp in range(1, N_GROUPS):
        upd = gsum[grp] > best
        best = jnp.where(upd, gsum[grp], best)
        best_g = jnp.where(upd, float(grp), best_g)
    rows = []
    for e in range(N_EXPERTS):
        grp = e // EXPERTS_PER_GROUP
        rows.append(jnp.where((best_g == float(grp)) & top2[e], s[e] / gsum[grp], 0.0))
    gt_ref[...] = jnp.concatenate(rows, axis=0)


def _outproj(y_r, y_l, y_s, x2, w_bf16, ln_g, ln_b, router_wt, router_b, tm=512):
    t, d = x2.shape
    row = lambda width: pl.BlockSpec((tm, width), lambda i: (i, 0))
    full = lambda shape: pl.BlockSpec(shape, lambda i: (0,) * len(shape))
    return pl.pallas_call(
        _outproj_kernel,
        grid=(t // tm,),
        in_specs=[row(RWKV_WIDTH), row(LRU_WIDTH), row(SBA_WIDTH), row(d), full((d, d)),
                  full((1, d)), full((1, d)), full((N_EXPERTS, d)), full((N_EXPERTS, 1))],
        out_specs=[row(d), pl.BlockSpec((N_EXPERTS, tm), lambda i: (0, i))],
        out_shape=[jax.ShapeDtypeStruct((t, d), F32), jax.ShapeDtypeStruct((N_EXPERTS, t), F32)],
        compiler_params=_params(("parallel",)),
        name="outproj_ln_router",
    )(y_r, y_l, y_s, x2, w_bf16, ln_g.reshape(1, d), ln_b.reshape(1, d), router_wt,
      router_b.reshape(N_EXPERTS, 1))


def _moe_kernel(x_ref, gate_ref, wg_ref, wu_ref, wd_ref, g_ref, b_ref, o_ref, xb_ref, acc_ref):
    e = pl.program_id(1)

    @pl.when(e == 0)
    def _():
        xb_ref[...] = x_ref[...].astype(BF16)
        acc_ref[...] = jnp.zeros_like(acc_ref)

    lane = lax.broadcasted_iota(jnp.int32, (1, N_EXPERTS), 1)
    gcol = jnp.sum(jnp.where(lane == e, gate_ref[...], 0.0), axis=1, keepdims=True)
    xb = xb_ref[...]
    hg = _dot(xb, wg_ref[0])
    hu = _dot(xb, wu_ref[0])
    h = hg * _sigmoid(hg) * hu * gcol
    acc_ref[...] += _dot(h.astype(BF16), wd_ref[0])

    @pl.when(e == N_EXPERTS - 1)
    def _():
        o_ref[...] = _layer_norm(ALPHA * x_ref[...] + acc_ref[...], g_ref[...], b_ref[...])


def _moe(x1, gates, wg, wu, wd, ln_g, ln_b, tm=1024):
    t, d = x1.shape
    de = wg.shape[2]
    tm = min(tm, t)
    return pl.pallas_call(
        _moe_kernel,
        grid=(t // tm, N_EXPERTS),
        in_specs=[pl.BlockSpec((tm, d), lambda i, e: (i, 0)),
                  pl.BlockSpec((tm, N_EXPERTS), lambda i, e: (i, 0)),
                  pl.BlockSpec((1, d, de), lambda i, e: (e, 0, 0)),
                  pl.BlockSpec((1, d, de), lambda i, e: (e, 0, 0)),
                  pl.BlockSpec((1, de, d), lambda i, e: (e, 0, 0)),
                  pl.BlockSpec((1, d), lambda i, e: (0, 0)),
                  pl.BlockSpec((1, d), lambda i, e: (0, 0))],
        out_specs=pl.BlockSpec((tm, d), lambda i, e: (i, 0)),
        out_shape=jax.ShapeDtypeStruct((t, d), F32),
        scratch_shapes=[pltpu.VMEM((tm, d), BF16), pltpu.VMEM((tm, d), F32)],
        compiler_params=_params(("parallel", "arbitrary")),
        name="moe_ln",
    )(x1, gates, wg, wu, wd, ln_g.reshape(1, d), ln_b.reshape(1, d))


def kernel(x, w_in, shift_mu, rwkv_w0, rwkv_w_up, rwkv_a0, rwkv_a_up, rwkv_g_up, rwkv_k_k, rwkv_k_a, rwkv_r_k, rwkv_ln_g, rwkv_ln_b, lru_conv_w, lru_conv_b, lru_wa, lru_ba, lru_wx, lru_bx, lru_lambda, lru_norm_g, sba_norm_g, w_out, ln1_g, ln1_b, ln2_g, ln2_b, router_w, router_b, exp_w_gate, exp_w_up, exp_w_down):
    bsz, seq, d = x.shape
    t = bsz * seq
    x2 = x.reshape(t, d)
    router_wt = router_w.T
    for l in range(w_in.shape[0]):
        p_r, p_l, p_q, p_kt, p_v = _inproj(x2, w_in[l].astype(BF16))
        y_r = _rwkv(p_r.reshape(bsz, seq, RWKV_COLS), shift_mu[l], rwkv_w0[l], rwkv_w_up[l],
                    rwkv_a0[l], rwkv_a_up[l], rwkv_g_up[l], rwkv_k_k[l], rwkv_k_a[l],
                    rwkv_r_k[l], rwkv_ln_g[l], rwkv_ln_b[l])
        y_l = _lru(p_l.reshape(bsz, seq, LRU_COLS), lru_conv_w[l], lru_conv_b[l], lru_wa[l],
                   lru_ba[l], lru_wx[l], lru_bx[l], lru_lambda[l], lru_norm_g[l])
        y_s = _sba(p_q.reshape(bsz, seq, SBA_WIDTH), p_kt, p_v.reshape(bsz, seq, SBA_WIDTH), sba_norm_g[l])
        x1, gates_t = _outproj(y_r.reshape(t, RWKV_WIDTH), y_l.reshape(t, LRU_WIDTH),
                               y_s.reshape(t, SBA_WIDTH), x2, w_out[l].astype(BF16),
                               ln1_g[l], ln1_b[l], router_wt, router_b)
        x2 = _moe(x1, gates_t.T, exp_w_gate[l].astype(BF16), exp_w_up[l].astype(BF16),
                  exp_w_down[l].astype(BF16), ln2_g[l], ln2_b[l])
    return x2.reshape(bsz, seq, d)
```

```python
import functools

import jax
import jax.numpy as jnp
from jax import lax
from jax.experimental import pallas as pl
from jax.experimental.pallas import tpu as pltpu

F32 = jnp.float32
BF16 = jnp.bfloat16

HEAD_DIM = 64
RWKV_WIDTH = 384
LRU_WIDTH = 256
SBA_WIDTH = 384
DECAY_RANK = 64
ICLR_RANK = 64
GATE_RANK = 128
RWKV_COLS = 3 * RWKV_WIDTH + DECAY_RANK + ICLR_RANK + GATE_RANK
LRU_COLS = 2 * LRU_WIDTH
SBA_COLS = 3 * SBA_WIDTH
RWKV_GN_EPS = 64e-5
CONV_WIDTH = 4
LRU_C = 8.0
N_EXPERTS = 16
N_GROUPS = 4
EXPERTS_PER_GROUP = 4
DEPTH = 4
ALPHA = (2.0 * DEPTH) ** 0.25
LN_EPS = 1e-5
RMS_EPS = 1e-6
LOG2E = 1.4426950408889634

LANES = 128
RWKV_CHUNK = 64
SBA_KEY_TILE = 256
ROUTE_TILE = 512
MOE_TILE = 1024
MOE_ROWS = 320
MOE_ALIGN = 16
VMEM_LIMIT = 48 * 1024 * 1024


def _dot(a, b):
    return jnp.dot(a, b, preferred_element_type=F32)


def _dot_nt(a, b):
    return lax.dot_general(a, b, (((1,), (1,)), ((), ())), preferred_element_type=F32)


def _dot_tn(a, b):
    return lax.dot_general(a, b, (((0,), (0,)), ((), ())), preferred_element_type=F32)


def _split3(x):
    hi = x.astype(BF16)
    r1 = x - hi.astype(F32)
    mid = r1.astype(BF16)
    lo = (r1 - mid.astype(F32)).astype(BF16)
    return hi, mid, lo


def _dot_sel_rhs(x, sel):
    hi, mid, lo = _split3(x)
    return _dot(hi, sel) + _dot(mid, sel) + _dot(lo, sel)


def _dot_sel_lhs(sel, x):
    hi, mid, lo = _split3(x)
    return _dot(sel, hi) + _dot(sel, mid) + _dot(sel, lo)


def _dot3(a, b, dot=_dot):
    ah, am, al = _split3(a)
    bh, bm, bl = _split3(b)
    return (dot(ah, bh) + (dot(ah, bm) + dot(am, bh))
            + (dot(ah, bl) + dot(am, bm) + dot(al, bh)))


def _softplus(x):
    return jnp.maximum(x, 0.0) + jnp.log1p(jnp.exp(-jnp.abs(x)))


def _sigmoid(x):
    return 1.0 / (1.0 + jnp.exp(-x))


def _params(sem):
    return pltpu.CompilerParams(dimension_semantics=sem, vmem_limit_bytes=VMEM_LIMIT)


def _inproj_kernel(x_ref, w_ref, wkt_ref, o_r, o_l, o_q, o_kt, o_v):
    xb = x_ref[...].astype(BF16)
    col = 0
    for o_ref, skip in ((o_r, 0), (o_l, 0), (o_q, SBA_WIDTH), (o_v, 0)):
        width = o_ref.shape[1]
        for c0 in range(0, width, 384):
            c1 = min(c0 + 384, width)
            o_ref[:, c0:c1] = _dot(xb, w_ref[:, col + c0:col + c1]).astype(o_ref.dtype)
        col += width + skip
    kt = _dot_nt(wkt_ref[...], xb)
    tk = o_kt.shape[2]
    for i in range(o_kt.shape[0]):
        o_kt[i] = kt[:, i * tk:(i + 1) * tk].astype(o_kt.dtype)


def _inproj(x2, w_bf16, tm=512, tk=SBA_KEY_TILE):
    t, d = x2.shape
    n = w_bf16.shape[1]
    k0 = RWKV_COLS + LRU_COLS + SBA_WIDTH
    wkt = w_bf16[:, k0:k0 + SBA_WIDTH].T
    return pl.pallas_call(
        _inproj_kernel,
        grid=(t // tm,),
        in_specs=[pl.BlockSpec((tm, d), lambda i: (i, 0)),
                  pl.BlockSpec((d, n), lambda i: (0, 0)),
                  pl.BlockSpec((SBA_WIDTH, d), lambda i: (0, 0))],
        out_specs=[pl.BlockSpec((tm, RWKV_COLS), lambda i: (i, 0)),
                   pl.BlockSpec((tm, LRU_COLS), lambda i: (i, 0)),
                   pl.BlockSpec((tm, SBA_WIDTH), lambda i: (i, 0)),
                   pl.BlockSpec((tm // tk, SBA_WIDTH, tk), lambda i: (i, 0, 0)),
                   pl.BlockSpec((tm, SBA_WIDTH), lambda i: (i, 0))],
        out_shape=[jax.ShapeDtypeStruct((t, RWKV_COLS), F32),
                   jax.ShapeDtypeStruct((t, LRU_COLS), F32),
                   jax.ShapeDtypeStruct((t, SBA_WIDTH), BF16),
                   jax.ShapeDtypeStruct((t // tk, SBA_WIDTH, tk), BF16),
                   jax.ShapeDtypeStruct((t, SBA_WIDTH), BF16)],
        compiler_params=_params(("parallel",)),
        name="inproj",
    )(x2, w_bf16, wkt)


def _rwkv_kernel(p_ref, mu_ref, w0_ref, a0_ref, kk_ref, ka_ref, rk_ref, lng_ref, lnb_ref,
                 wup_ref, aup_ref, gup_ref, ones_ref, o_ref, carry_ref, ht_ref, *, nb):
    c = RWKV_CHUNK
    w = RWKV_WIDTH
    rows = nb * c
    t = pl.program_id(1)

    @pl.when(t == 0)
    def _():
        carry_ref[...] = jnp.zeros_like(carry_ref)
        ht_ref[...] = jnp.zeros_like(ht_ref)

    p = p_ref[...].reshape(rows, RWKV_COLS)
    row = lax.broadcasted_iota(jnp.int32, (rows, 1), 0)
    prev = pltpu.roll(p, 1, axis=0)
    for n in range(nb):
        prev = jnp.where(row == n * c, carry_ref[n:n + 1, :], prev)
    for n in range(nb):
        carry_ref[n:n + 1, :] = p_ref[n, c - 1:c, :]
    ps = p + mu_ref[...] * (prev - p)

    r = ps[:, 0:w]
    k = ps[:, w:2 * w]
    v = ps[:, 2 * w:3 * w]
    lora_in = ps[:, 3 * w:3 * w + LANES]
    dg = ps[:, 3 * w + LANES:3 * w + 2 * LANES]

    w_log = -_softplus(-(w0_ref[...] + _dot(jnp.tanh(lora_in).astype(BF16), wup_ref[...]))) - 0.5
    logw = -jnp.exp(w_log)
    a = _sigmoid(a0_ref[...] + _dot(lora_in.astype(BF16), aup_ref[...]))
    g = _dot(_sigmoid(dg).astype(BF16), gup_ref[...])

    ones = ones_ref[...]

    def seg_sum(x):
        return jnp.concatenate([_dot(x[:, i:i + LANES].astype(BF16), ones) for i in range(0, w, LANES)], axis=1)

    kkr = k * kk_ref[...]
    kk = kkr / jnp.maximum(jnp.sqrt(seg_sum(kkr * kkr)), 1e-12)
    k2 = k * (1.0 + (a - 1.0) * ka_ref[...])
    b = kk * a

    ri = lax.broadcasted_iota(jnp.int32, (rows, rows), 0)
    ci = lax.broadcasted_iota(jnp.int32, (rows, rows), 1)
    tri = jnp.where(((ri // c) == (ci // c)) & (ci <= ri), 1.0, 0.0).astype(BF16)
    lw_hi = logw.astype(BF16)
    lw_lo = (logw - lw_hi.astype(F32)).astype(BF16)
    cum = _dot(tri, lw_hi) + _dot(tri, lw_lo)

    lane = lax.broadcasted_iota(jnp.int32, (1, LANES), 1)
    first = lane < HEAD_DIM

    def stack(x):
        return jnp.concatenate([jnp.where(first, x, 0.0), jnp.where(first, 0.0, x)], axis=0)

    ri2 = lax.broadcasted_iota(jnp.int32, (2 * c, 2 * c), 0)
    ci2 = lax.broadcasted_iota(jnp.int32, (2 * c, 2 * c), 1)
    same = (ri2 // c) == (ci2 // c)
    strict = same & (ci2 < ri2)
    incl = same & (ci2 <= ri2)
    eye = jnp.where(ri2 == ci2, 1.0, 0.0)

    def dotb(x, y):
        return _dot(x.astype(BF16), y.astype(BF16))

    cum_tot = jnp.concatenate(
        [jnp.broadcast_to(cum[n * c + c - 1:n * c + c, :], (c, w)) for n in range(nb)], axis=0)
    rt = r * jnp.exp(cum)
    at = -kk * jnp.exp(cum - logw)
    e_neg = jnp.exp(-cum)
    bt = b * e_neg
    kt = k2 * e_neg
    e_rem = jnp.exp(cum_tot - cum)
    bh = b * e_rem
    kh = k2 * e_rem
    e_tot = jnp.exp(cum_tot)

    chains = [(n, j) for n in range(nb) for j in range(w // LANES)]

    def cut(x, n, j):
        return x[n * c:(n + 1) * c, j * LANES:(j + 1) * LANES]

    ar = [jnp.concatenate([stack(cut(at, n, j)), stack(cut(rt, n, j))], axis=0).astype(BF16)
          for n, j in chains]
    bk = [jnp.concatenate([stack(cut(bt, n, j)), stack(cut(kt, n, j))], axis=0).astype(BF16)
          for n, j in chains]
    gram = [_dot_nt(x, y) for x, y in zip(ar, bk)]
    ht = [ht_ref[n, j] for n, j in chains]
    xh = [_dot_nt(x, h.astype(BF16)) for x, h in zip(ar, ht)]
    v_b = [stack(cut(v, n, j)).astype(BF16) for n, j in chains]
    a_ab = [jnp.where(strict, gm[0:2 * c, 0:2 * c], 0.0) for gm in gram]
    a_ak = [jnp.where(strict, gm[0:2 * c, 2 * c:4 * c], 0.0).astype(BF16) for gm in gram]
    a_r = [jnp.concatenate([jnp.where(incl, gm[2 * c:4 * c, 0:2 * c], 0.0),
                            jnp.where(incl, gm[2 * c:4 * c, 2 * c:4 * c], 0.0)], axis=1).astype(BF16)
           for gm in gram]
    rhs_u = [x[0:2 * c] + _dot(ak, vb) for x, ak, vb in zip(xh, a_ak, v_b)]

    pw = [x.astype(BF16) for x in a_ab]
    inv = [eye + x for x in a_ab]
    for _ in range(5):
        pw_f = [_dot(x, x) for x in pw]
        pw = [x.astype(BF16) for x in pw_f]
        inv = [x + _dot(x.astype(BF16), y) for x, y in zip(inv, pw)]

    u_d = [dotb(x, y) for x, y in zip(inv, rhs_u)]
    uv = [jnp.concatenate([u.astype(BF16), vb], axis=0) for u, vb in zip(u_d, v_b)]
    o_d = [x[2 * c:4 * c] + _dot(a, y) for x, a, y in zip(xh, a_r, uv)]
    bkh = [jnp.concatenate([stack(cut(bh, n, j)), stack(cut(kh, n, j))], axis=0).astype(BF16)
           for n, j in chains]
    upd = [_dot_tn(y, z) for y, z in zip(uv, bkh)]
    for (n, j), h, u in zip(chains, ht, upd):
        ht_ref[n, j] = h * e_tot[n * c:n * c + 1, j * LANES:(j + 1) * LANES] + u
    o_pair = [x[0:c] + x[c:2 * c] for x in o_d]
    npairs = w // LANES
    o = jnp.concatenate([jnp.concatenate(o_pair[n * npairs:(n + 1) * npairs], axis=1)
                         for n in range(nb)], axis=0)
    inv_n = 1.0 / HEAD_DIM
    d = o - seg_sum(o) * inv_n
    var = seg_sum(d * d) * inv_n
    on = d * lax.rsqrt(var + RWKV_GN_EPS) * lng_ref[...] + lnb_ref[...]
    bonus = seg_sum(r * k2 * rk_ref[...]) * v
    o_ref[...] = ((on + bonus) * g).reshape(nb, c, w)


def _rwkv(p_r, mu, w0, wup, a0, aup, gup, k_k, k_a, r_k, ln_g, ln_b, nb=4):
    bsz, seq, _ = p_r.shape
    nb = min(nb, bsz)
    c = RWKV_CHUNK
    w = RWKV_WIDTH
    zeros = jnp.zeros((DECAY_RANK, w), F32)
    wup_pad = jnp.concatenate([wup, zeros], axis=0).astype(BF16)
    aup_pad = jnp.concatenate([zeros, aup], axis=0).astype(BF16)
    head = jnp.arange(LANES) // HEAD_DIM
    ones = (head[:, None] == head[None, :]).astype(BF16)
    vec = lambda x: x.reshape(1, -1)
    full = lambda shape: pl.BlockSpec(shape, lambda b, t: (0,) * len(shape))
    return pl.pallas_call(
        functools.partial(_rwkv_kernel, nb=nb),
        grid=(bsz // nb, seq // c),
        in_specs=[pl.BlockSpec((nb, c, RWKV_COLS), lambda b, t: (b, t, 0)),
                  full((1, RWKV_COLS))] + [full((1, w))] * 7
                 + [full((LANES, w)), full((LANES, w)), full((GATE_RANK, w)), full((LANES, LANES))],
        out_specs=pl.BlockSpec((nb, c, w), lambda b, t: (b, t, 0)),
        out_shape=jax.ShapeDtypeStruct((bsz, seq, w), F32),
        scratch_shapes=[pltpu.VMEM((8, RWKV_COLS), F32),
                        pltpu.VMEM((nb, w // LANES, LANES, LANES), F32)],
        compiler_params=_params(("parallel", "arbitrary")),
        name="rwkv7",
    )(p_r, vec(mu), vec(w0), vec(a0), vec(k_k), vec(k_a), vec(r_k), vec(ln_g), vec(ln_b),
      wup_pad, aup_pad, gup.astype(BF16), ones)


def _lru_kernel(p_ref, cw_ref, cb_ref, wax_ref, bax_ref, lam_ref, ng_ref, ones_ref, o_ref,
                xs_ref, h_ref):
    tb = p_ref.shape[1]
    w = LRU_WIDTH
    hist = 8
    t = pl.program_id(1)

    @pl.when(t == 0)
    def _():
        xs_ref[0:hist, :] = jnp.zeros((hist, w), F32)
        h_ref[...] = jnp.zeros_like(h_ref)

    @pl.when(t > 0)
    def _():
        xs_ref[0:hist, :] = xs_ref[tb:tb + hist, :]

    gate_in = p_ref[0, :, 0:w]
    xin = p_ref[0, :, w:2 * w]
    xs_ref[hist:hist + tb, :] = xin
    xc = cb_ref[...] + cw_ref[CONV_WIDTH - 1:CONV_WIDTH, :] * xin
    for j in range(1, CONV_WIDTH):
        xc = xc + cw_ref[CONV_WIDTH - 1 - j:CONV_WIDTH - j, :] * xs_ref[hist - j:hist - j + tb, :]

    gates = _dot(xc.astype(BF16), wax_ref[...]) + bax_ref[...]
    gate_a = _sigmoid(gates[:, 0:w])
    gate_x = _sigmoid(gates[:, w:2 * w])
    log_a = -LRU_C * gate_a * _softplus(-lam_ref[...])
    a = jnp.exp(log_a)
    bb = jnp.sqrt(1.0 - jnp.exp(2.0 * log_a)) * (gate_x * xc)

    row = lax.broadcasted_iota(jnp.int32, (tb, 1), 0)
    d = 1
    while d < tb:
        keep = row >= d
        a_s = jnp.where(keep, pltpu.roll(a, d, axis=0), 1.0)
        b_s = jnp.where(keep, pltpu.roll(bb, d, axis=0), 0.0)
        bb = a * b_s + bb
        a = a * a_s
        d *= 2
    h = bb + a * h_ref[0:1, :]
    h_ref[0:1, :] = h[tb - 1:tb, :]

    gelu = 0.5 * gate_in * (1.0 + jnp.tanh(0.7978845608028654 * (gate_in + 0.044715 * (gate_in * gate_in * gate_in))))
    y = gelu * h
    ms = _dot((y * y).astype(BF16), ones_ref[...]) * (1.0 / HEAD_DIM)
    o_ref[0] = y * lax.rsqrt(ms + RMS_EPS) * ng_ref[...]


def _block_diag(wb):
    n, d, _ = wb.shape
    eye = jnp.eye(n, dtype=wb.dtype)
    return jnp.einsum('gij,gh->gihj', wb, eye).reshape(n * d, n * d)


def _lru(p_l, conv_w, conv_b, wa, ba, wx, bx, lam, norm_g, tb=512):
    bsz, seq, _ = p_l.shape
    w = LRU_WIDTH
    tb = min(tb, seq)
    wax = jnp.concatenate([_block_diag(wa), _block_diag(wx)], axis=1).astype(BF16)
    bax = jnp.concatenate([ba, bx]).reshape(1, 2 * w)
    head = jnp.arange(w) // HEAD_DIM
    ones = (head[:, None] == head[None, :]).astype(BF16)
    full = lambda shape: pl.BlockSpec(shape, lambda b, t: (0,) * len(shape))
    return pl.pallas_call(
        _lru_kernel,
        grid=(bsz, seq // tb),
        in_specs=[pl.BlockSpec((1, tb, LRU_COLS), lambda b, t: (b, t, 0)),
                  full((CONV_WIDTH, w)), full((1, w)), full((w, 2 * w)), full((1, 2 * w)),
                  full((1, w)), full((1, w)), full((w, w))],
        out_specs=pl.BlockSpec((1, tb, w), lambda b, t: (b, t, 0)),
        out_shape=jax.ShapeDtypeStruct((bsz, seq, w), F32),
        scratch_shapes=[pltpu.VMEM((tb + 8, w), F32), pltpu.VMEM((8, w), F32)],
        compiler_params=_params(("parallel", "arbitrary")),
        name="rglru",
    )(p_l, conv_w, conv_b.reshape(1, w), wax, bax, lam.reshape(1, w), norm_g.reshape(1, w), ones)


def _sba_kernel(q_ref, kt_ref, v_ref, g_ref, cs_ref, ones_ref, o_ref, acc_ref, car_ref, *, tq, tk):
    qi = pl.program_id(2)
    nsub = tq // tk
    lane = lax.broadcasted_iota(jnp.int32, (1, LANES), 1)
    first = lane < HEAD_DIM
    q = q_ref[0] * jnp.asarray(HEAD_DIM ** -0.5, BF16)
    zero = jnp.zeros_like(q)
    qh = (jnp.where(first, q, zero), jnp.where(first, zero, q))
    cs = cs_ref[...]
    acc_ref[...] = jnp.zeros_like(acc_ref)
    car_ref[...] = jnp.zeros_like(car_ref)

    def tile(j, r0, masked):
        kt = kt_ref[j]
        vt = v_ref[0, pl.ds(pl.multiple_of(j * tk, tk), tk), :]
        if masked:
            ti = lax.broadcasted_iota(jnp.int32, (tq - r0, tk), 0)
            si = lax.broadcasted_iota(jnp.int32, (tq - r0, tk), 1)
            mask = si < ti
        for h in range(2):
            z = _dot(qh[h][r0:], kt)
            nk = jnp.maximum(z, 0.0) + jnp.log(1.0 + jnp.exp2(jnp.abs(z) * -LOG2E))
            if masked:
                nk = jnp.where(mask, nk, 0.0)
            after = _dot(nk.astype(BF16), cs)
            car = car_ref[h, r0:, :]
            attn = jnp.exp((z - nk) + after + car)
            if masked:
                attn = jnp.where(mask, attn, 0.0)
            acc_ref[h, r0:, :] += _dot(attn.astype(BF16), vt)
            car_ref[h, r0:, :] = car + (after[:, 0:1] - nk[:, 0:1])

    for d in range(nsub - 1, -1, -1):
        tile(qi * nsub + d, d * tk, True)

    def body(i, _):
        tile(qi * nsub - 1 - i, 0, False)
        return 0

    lax.fori_loop(0, qi * nsub, body, 0)
    o = jnp.where(first, acc_ref[0], acc_ref[1])
    ms = _dot((o * o).astype(BF16), ones_ref[...]) * (1.0 / HEAD_DIM)
    o_ref[0] = o * lax.rsqrt(ms + RMS_EPS) * g_ref[...]


def _sba(q, kt, v, norm_g, tq=1024):
    bsz, seq, _ = q.shape
    tk = kt.shape[2]
    tq = min(tq, seq)
    pairs = SBA_WIDTH // LANES
    ji = jnp.arange(tk)
    cs = -(ji[:, None] > ji[None, :]).astype(BF16)
    head = jnp.arange(LANES) // HEAD_DIM
    ones = (head[:, None] == head[None, :]).astype(BF16)
    return pl.pallas_call(
        functools.partial(_sba_kernel, tq=tq, tk=tk),
        grid=(bsz, pairs, seq // tq),
        in_specs=[pl.BlockSpec((1, tq, LANES), lambda b, h, i: (b, i, h)),
                  pl.BlockSpec((seq // tk, LANES, tk), lambda b, h, i: (b, h, 0)),
                  pl.BlockSpec((1, seq, LANES), lambda b, h, i: (b, 0, h)),
                  pl.BlockSpec((1, LANES), lambda b, h, i: (0, h)),
                  pl.BlockSpec((tk, tk), lambda b, h, i: (0, 0)),
                  pl.BlockSpec((LANES, LANES), lambda b, h, i: (0, 0))],
        out_specs=pl.BlockSpec((1, tq, LANES), lambda b, h, i: (b, i, h)),
        out_shape=jax.ShapeDtypeStruct((bsz, seq, SBA_WIDTH), F32),
        scratch_shapes=[pltpu.VMEM((2, tq, LANES), F32), pltpu.VMEM((2, tq, 1), F32)],
        compiler_params=_params(("parallel", "parallel", "arbitrary")),
        name="stickbreak",
    )(q, kt, v, norm_g.reshape(1, SBA_WIDTH), cs, ones)


def _layer_norm(h, g, b):
    mu = jnp.mean(h, axis=-1, keepdims=True)
    d = h - mu
    var = jnp.mean(d * d, axis=-1, keepdims=True)
    return d * lax.rsqrt(var + LN_EPS) * g + b


def _outproj_kernel(yr_ref, yl_ref, ys_ref, x_ref, w_ref, g_ref, b_ref, rw_ref, rb_ref,
                    x1_ref, g4_ref, grp_ref, cnt_ref):
    r0, r1 = RWKV_WIDTH, RWKV_WIDTH + LRU_WIDTH
    mix = (_dot(yr_ref[...].astype(BF16), w_ref[0:r0, :])
           + _dot(yl_ref[...].astype(BF16), w_ref[r0:r1, :])
           + _dot(ys_ref[...].astype(BF16), w_ref[r1:, :]))
    x1 = _layer_norm(ALPHA * x_ref[...] + mix, g_ref[...], b_ref[...])
    x1_ref[...] = x1

    logits = _dot3(rw_ref[...], x1, dot=_dot_nt) + rb_ref[...]
    mx = jnp.max(logits, axis=0, keepdims=True)
    ex = jnp.exp(logits - mx)
    scores = ex / jnp.sum(ex, axis=0, keepdims=True)
    s = [scores[e:e + 1, :] for e in range(N_EXPERTS)]
    top2 = []
    gsum = []
    for grp in range(N_GROUPS):
        mem = list(range(grp * EXPERTS_PER_GROUP, (grp + 1) * EXPERTS_PER_GROUP))
        tot = None
        for e in mem:
            rank = None
            for o in mem:
                if o == e:
                    continue
                ahead = (s[o] > s[e]) | ((s[o] == s[e]) if o < e else False)
                ahead = jnp.where(ahead, 1.0, 0.0)
                rank = ahead if rank is None else rank + ahead
            sel = rank < 1.5
            top2.append(sel)
            part = jnp.where(sel, s[e], 0.0)
            tot = part if tot is None else tot + part
        gsum.append(tot)
    best = gsum[0]
    best_g = jnp.zeros_like(best)
    for grp in range(1, N_GROUPS):
        upd = gsum[grp] > best
        best = jnp.where(upd, gsum[grp], best)
        best_g = jnp.where(upd, float(grp), best_g)
    rows = []
    for k in range(EXPERTS_PER_GROUP):
        acc = None
        for grp in range(N_GROUPS):
            e = grp * EXPERTS_PER_GROUP + k
            part = jnp.where((best_g == float(grp)) & top2[e], s[e] / gsum[grp], 0.0)
            acc = part if acc is None else acc + part
        rows.append(acc)
    g4_ref[...] = jnp.concatenate(rows, axis=0)
    grp_ref[...] = best_g.astype(jnp.int32)
    lane = lax.broadcasted_iota(jnp.int32, (1, LANES), 1)
    cnt = jnp.zeros((1, LANES), F32)
    for grp in range(N_GROUPS):
        n_grp = jnp.sum(jnp.where(best_g == float(grp), 1.0, 0.0), axis=1, keepdims=True)
        cnt = cnt + jnp.where(lane == grp, n_grp, 0.0)
    cnt_ref[0] = cnt.astype(jnp.int32)


def _outproj(y_r, y_l, y_s, x2, w_bf16, ln_g, ln_b, router_wt, router_b, tm=ROUTE_TILE):
    t, d = x2.shape
    tm = min(tm, t)
    row = lambda width: pl.BlockSpec((tm, width), lambda i: (i, 0))
    full = lambda shape: pl.BlockSpec(shape, lambda i: (0,) * len(shape))
    return pl.pallas_call(
        _outproj_kernel,
        grid=(t // tm,),
        in_specs=[row(RWKV_WIDTH), row(LRU_WIDTH), row(SBA_WIDTH), row(d), full((d, d)),
                  full((1, d)), full((1, d)), full((N_EXPERTS, d)), full((N_EXPERTS, 1))],
        out_specs=[row(d), pl.BlockSpec((EXPERTS_PER_GROUP, tm), lambda i: (0, i)),
                   pl.BlockSpec((1, tm), lambda i: (0, i)),
                   pl.BlockSpec((1, 1, LANES), lambda i: (i, 0, 0))],
        out_shape=[jax.ShapeDtypeStruct((t, d), F32),
                   jax.ShapeDtypeStruct((EXPERTS_PER_GROUP, t), F32),
                   jax.ShapeDtypeStruct((1, t), jnp.int32),
                   jax.ShapeDtypeStruct((t // tm, 1, LANES), jnp.int32)],
        compiler_params=_params(("parallel",)),
        name="outproj_ln_router",
    )(y_r, y_l, y_s, x2, w_bf16, ln_g.reshape(1, d), ln_b.reshape(1, d), router_wt,
      router_b.reshape(N_EXPERTS, 1))


def _moe_kernel(cnt_ref, x_ref, grp_ref, g4_ref, wg_ref, wu_ref, wd_ref, g_ref, b_ref, o_ref,
                xs_ref, gs_ref, ys_ref, dest_ref, *, tm, rb, nsub):
    i = pl.program_id(0)
    e = pl.program_id(1)
    cap = xs_ref.shape[0]
    chunks = tm // LANES
    n = [sum(cnt_ref[i * nsub + s, grp] for s in range(nsub)) for grp in range(N_GROUPS)]
    start = [0]
    for grp in range(1, N_GROUPS):
        start.append(start[-1] + n[grp - 1])

    @pl.when(e == 0)
    def _():
        grp = grp_ref[...]
        onehot = jnp.concatenate([jnp.where(grp == k, 1.0, 0.0) for k in range(N_GROUPS)], axis=0)
        li = lax.broadcasted_iota(jnp.int32, (LANES, LANES), 0)
        lj = lax.broadcasted_iota(jnp.int32, (LANES, LANES), 1)
        before = jnp.where(li < lj, 1.0, 0.0).astype(BF16)
        within = _dot(onehot.astype(BF16), before)
        tot = jnp.broadcast_to(jnp.sum(onehot, axis=1, keepdims=True), onehot.shape)
        ri = lax.broadcasted_iota(jnp.int32, (N_GROUPS * chunks, N_GROUPS * chunks), 0)
        ci = lax.broadcasted_iota(jnp.int32, (N_GROUPS * chunks, N_GROUPS * chunks), 1)
        earlier = jnp.where(((ri // chunks) == (ci // chunks)) & (ci < ri), 1.0, 0.0).astype(BF16)
        rank = within + _dot(earlier, tot.astype(BF16))
        dest = jnp.zeros((chunks, LANES), F32)
        for k in range(N_GROUPS):
            dest = jnp.where(grp == k, rank[k * chunks:(k + 1) * chunks] + jnp.asarray(start[k]).astype(F32), dest)
        dest_ref[...] = dest
        dest_i = dest.astype(jnp.int32)
        r_iota = lax.broadcasted_iota(jnp.int32, (tm, LANES), 0)
        place = jnp.concatenate(
            [jnp.where(r_iota == dest_i[c:c + 1, :], 1.0, 0.0).astype(BF16) for c in range(chunks)],
            axis=1)
        xs_ref[0:tm, :] = _dot(place, x_ref[...].astype(BF16)).astype(BF16)
        xs_ref[tm:cap, :] = jnp.zeros((cap - tm, xs_ref.shape[1]), BF16)
        pieces = [p.astype(F32) for p in _split3(g4_ref[...])]
        gates_t = jnp.transpose(jnp.concatenate(
            pieces + [jnp.zeros((LANES - 3 * EXPERTS_PER_GROUP, tm), F32)], axis=0))
        gs_ref[0:tm, :] = _dot(place, gates_t.astype(BF16))
        gs_ref[tm:cap, :] = jnp.zeros((cap - tm, LANES), F32)
        ys_ref[...] = jnp.zeros_like(ys_ref)

    grp_e = e // EXPERTS_PER_GROUP
    k_e = e % EXPERTS_PER_GROUP
    n_e = n[0]
    s_e = start[0]
    for grp in range(1, N_GROUPS):
        n_e = jnp.where(grp_e == grp, n[grp], n_e)
        s_e = jnp.where(grp_e == grp, start[grp], s_e)
    lane = lax.broadcasted_iota(jnp.int32, (1, LANES), 1)
    mine = ((lane % EXPERTS_PER_GROUP) == k_e) & (lane < 3 * EXPERTS_PER_GROUP)
    s_al = s_e // MOE_ALIGN * MOE_ALIGN

    def block(b, carry):
        r0 = pl.multiple_of(s_al + b * rb, MOE_ALIGN)
        xb = xs_ref[pl.ds(r0, rb), :]
        gate = jnp.sum(jnp.where(mine, gs_ref[pl.ds(r0, rb), :], 0.0), axis=1, keepdims=True)
        rid = r0 + lax.broadcasted_iota(jnp.int32, (rb, 1), 0)
        gate = jnp.where((rid >= s_e) & (rid < s_e + n_e), gate, 0.0)
        hg = _dot(xb, wg_ref[0])
        hu = _dot(xb, wu_ref[0])
        h = hg * _sigmoid(hg) * hu * gate
        ys_ref[pl.ds(r0, rb), :] += _dot(h.astype(BF16), wd_ref[0])
        return carry

    lax.fori_loop(0, jnp.where(n_e > 0, (s_e + n_e - s_al + rb - 1) // rb, 0), block, 0)

    @pl.when(e == N_EXPERTS - 1)
    def _():
        dest = dest_ref[...]
        dest_t = jnp.transpose(jnp.concatenate([dest, jnp.zeros((LANES - chunks, LANES), F32)], axis=0))
        dest_t = dest_t.astype(jnp.int32)
        c_iota = lax.broadcasted_iota(jnp.int32, (LANES, tm), 1)
        unplace = jnp.concatenate(
            [jnp.where(c_iota == dest_t[:, c:c + 1], 1.0, 0.0).astype(BF16) for c in range(chunks)],
            axis=0)
        moe = _dot(unplace, ys_ref[0:tm, :].astype(BF16))
        o_ref[...] = _layer_norm(ALPHA * x_ref[...] + moe, g_ref[...], b_ref[...])


def _moe(x1, g4, grp, cnt, wg, wu, wd, ln_g, ln_b, tm=MOE_TILE, rb=MOE_ROWS):
    t, d = x1.shape
    de = wg.shape[2]
    tm = min(tm, t)
    nsub = cnt.shape[0] * tm // t
    cap = tm + rb
    grid_spec = pltpu.PrefetchScalarGridSpec(
        num_scalar_prefetch=1,
        grid=(t // tm, N_EXPERTS),
        in_specs=[pl.BlockSpec((tm, d), lambda i, e, c: (i, 0)),
                  pl.BlockSpec((tm // LANES, LANES), lambda i, e, c: (i, 0)),
                  pl.BlockSpec((EXPERTS_PER_GROUP, tm), lambda i, e, c: (0, i)),
                  pl.BlockSpec((1, d, de), lambda i, e, c: (e, 0, 0)),
                  pl.BlockSpec((1, d, de), lambda i, e, c: (e, 0, 0)),
                  pl.BlockSpec((1, de, d), lambda i, e, c: (e, 0, 0)),
                  pl.BlockSpec((1, d), lambda i, e, c: (0, 0)),
                  pl.BlockSpec((1, d), lambda i, e, c: (0, 0))],
        out_specs=pl.BlockSpec((tm, d), lambda i, e, c: (i, 0)),
        scratch_shapes=[pltpu.VMEM((cap, d), BF16), pltpu.VMEM((cap, LANES), F32),
                        pltpu.VMEM((cap, d), F32), pltpu.VMEM((tm // LANES, LANES), F32)])
    return pl.pallas_call(
        functools.partial(_moe_kernel, tm=tm, rb=rb, nsub=nsub),
        grid_spec=grid_spec,
        out_shape=jax.ShapeDtypeStruct((t, d), F32),
        compiler_params=_params(("parallel", "arbitrary")),
        name="moe_ln",
    )(cnt.reshape(cnt.shape[0], LANES)[:, :N_GROUPS], x1, grp.reshape(t // LANES, LANES), g4,
      wg, wu, wd, ln_g.reshape(1, d), ln_b.reshape(1, d))


def kernel(x, w_in, shift_mu, rwkv_w0, rwkv_w_up, rwkv_a0, rwkv_a_up, rwkv_g_up, rwkv_k_k, rwkv_k_a, rwkv_r_k, rwkv_ln_g, rwkv_ln_b, lru_conv_w, lru_conv_b, lru_wa, lru_ba, lru_wx, lru_bx, lru_lambda, lru_norm_g, sba_norm_g, w_out, ln1_g, ln1_b, ln2_g, ln2_b, router_w, router_b, exp_w_gate, exp_w_up, exp_w_down):
    bsz, seq, d = x.shape
    t = bsz * seq
    x2 = x.reshape(t, d)
    router_wt = router_w.T
    for l in range(w_in.shape[0]):
        p_r, p_l, p_q, p_kt, p_v = _inproj(x2, w_in[l].astype(BF16))
        y_r = _rwkv(p_r.reshape(bsz, seq, RWKV_COLS), shift_mu[l], rwkv_w0[l], rwkv_w_up[l],
                    rwkv_a0[l], rwkv_a_up[l], rwkv_g_up[l], rwkv_k_k[l], rwkv_k_a[l],
                    rwkv_r_k[l], rwkv_ln_g[l], rwkv_ln_b[l])
        y_l = _lru(p_l.reshape(bsz, seq, LRU_COLS), lru_conv_w[l], lru_conv_b[l], lru_wa[l],
                   lru_ba[l], lru_wx[l], lru_bx[l], lru_lambda[l], lru_norm_g[l])
        y_s = _sba(p_q.reshape(bsz, seq, SBA_WIDTH), p_kt, p_v.reshape(bsz, seq, SBA_WIDTH), sba_norm_g[l])
        x1, g4, grp, cnt = _outproj(y_r.reshape(t, RWKV_WIDTH), y_l.reshape(t, LRU_WIDTH),
                                    y_s.reshape(t, SBA_WIDTH), x2, w_out[l].astype(BF16),
                                    ln1_g[l], ln1_b[l], router_wt, router_b)
        x2 = _moe(x1, g4, grp, cnt, exp_w_gate[l].astype(BF16), exp_w_up[l].astype(BF16),
                  exp_w_down[l].astype(BF16), ln2_g[l], ln2_b[l])
    return x2.reshape(bsz, seq, d)
```

```python
import functools

import jax
import jax.numpy as jnp
from jax import lax
from jax.experimental import pallas as pl
from jax.experimental.pallas import tpu as pltpu

F32 = jnp.float32
BF16 = jnp.bfloat16

HEAD_DIM = 64
RWKV_WIDTH = 384
LRU_WIDTH = 256
SBA_WIDTH = 384
DECAY_RANK = 64
ICLR_RANK = 64
GATE_RANK = 128
RWKV_COLS = 3 * RWKV_WIDTH + DECAY_RANK + ICLR_RANK + GATE_RANK
LRU_COLS = 2 * LRU_WIDTH
SBA_COLS = 3 * SBA_WIDTH
RWKV_GN_EPS = 64e-5
CONV_WIDTH = 4
LRU_C = 8.0
N_EXPERTS = 16
N_GROUPS = 4
EXPERTS_PER_GROUP = 4
DEPTH = 4
ALPHA = (2.0 * DEPTH) ** 0.25
LN_EPS = 1e-5
RMS_EPS = 1e-6
LOG2E = 1.4426950408889634
SIGN_BIT = -2147483648
SBA_DEAD_LOG2 = -160.0

LANES = 128
RWKV_CHUNK = 64
SBA_KEY_TILE = 256
ROUTE_TILE = 512
MOE_TILE = 1024
MOE_ROWS = 288
MOE_ALIGN = 16
VMEM_LIMIT = 48 * 1024 * 1024


def _dot(a, b):
    return jnp.dot(a, b, preferred_element_type=F32)


def _dot_nt(a, b):
    return lax.dot_general(a, b, (((1,), (1,)), ((), ())), preferred_element_type=F32)


def _dot_tn(a, b):
    return lax.dot_general(a, b, (((0,), (0,)), ((), ())), preferred_element_type=F32)


def _split3(x):
    hi = x.astype(BF16)
    r1 = x - hi.astype(F32)
    mid = r1.astype(BF16)
    lo = (r1 - mid.astype(F32)).astype(BF16)
    return hi, mid, lo


def _dot_sel_rhs(x, sel):
    hi, mid, lo = _split3(x)
    return _dot(hi, sel) + _dot(mid, sel) + _dot(lo, sel)


def _dot_sel_lhs(sel, x):
    hi, mid, lo = _split3(x)
    return _dot(sel, hi) + _dot(sel, mid) + _dot(sel, lo)


def _dot3(a, b, dot=_dot):
    ah, am, al = _split3(a)
    bh, bm, bl = _split3(b)
    return (dot(ah, bh) + (dot(ah, bm) + dot(am, bh))
            + (dot(ah, bl) + dot(am, bm) + dot(al, bh)))


def _softplus(x):
    return jnp.maximum(x, 0.0) + jnp.log1p(jnp.exp(-jnp.abs(x)))


def _sigmoid(x):
    return 1.0 / (1.0 + jnp.exp(-x))


def _params(sem):
    return pltpu.CompilerParams(dimension_semantics=sem, vmem_limit_bytes=VMEM_LIMIT)


def _inproj_kernel(x_ref, w_ref, wkt_ref, o_r, o_l, o_q, o_kt, o_v):
    xb = x_ref[...].astype(BF16)
    col = 0
    for o_ref, skip in ((o_r, 0), (o_l, 0), (o_q, SBA_WIDTH), (o_v, 0)):
        width = o_ref.shape[1]
        for c0 in range(0, width, 384):
            c1 = min(c0 + 384, width)
            o_ref[:, c0:c1] = _dot(xb, w_ref[:, col + c0:col + c1]).astype(o_ref.dtype)
        col += width + skip
    kt = _dot_nt(wkt_ref[...], xb)
    tk = o_kt.shape[2]
    for i in range(o_kt.shape[0]):
        o_kt[i] = kt[:, i * tk:(i + 1) * tk].astype(o_kt.dtype)


def _inproj(x2, w, tm=512, tk=SBA_KEY_TILE):
    t, d = x2.shape
    n = w.shape[1]
    q0 = RWKV_COLS + LRU_COLS
    k0 = q0 + SBA_WIDTH
    col = jnp.arange(n)
    w_bf16 = (w * jnp.where((col >= q0) & (col < k0), LOG2E * HEAD_DIM ** -0.5, 1.0)).astype(BF16)
    wkt = w_bf16[:, k0:k0 + SBA_WIDTH].T
    return pl.pallas_call(
        _inproj_kernel,
        grid=(t // tm,),
        in_specs=[pl.BlockSpec((tm, d), lambda i: (i, 0)),
                  pl.BlockSpec((d, n), lambda i: (0, 0)),
                  pl.BlockSpec((SBA_WIDTH, d), lambda i: (0, 0))],
        out_specs=[pl.BlockSpec((tm, RWKV_COLS), lambda i: (i, 0)),
                   pl.BlockSpec((tm, LRU_COLS), lambda i: (i, 0)),
                   pl.BlockSpec((tm, SBA_WIDTH), lambda i: (i, 0)),
                   pl.BlockSpec((tm // tk, SBA_WIDTH, tk), lambda i: (i, 0, 0)),
                   pl.BlockSpec((tm, SBA_WIDTH), lambda i: (i, 0))],
        out_shape=[jax.ShapeDtypeStruct((t, RWKV_COLS), F32),
                   jax.ShapeDtypeStruct((t, LRU_COLS), F32),
                   jax.ShapeDtypeStruct((t, SBA_WIDTH), BF16),
                   jax.ShapeDtypeStruct((t // tk, SBA_WIDTH, tk), BF16),
                   jax.ShapeDtypeStruct((t, SBA_WIDTH), BF16)],
        compiler_params=_params(("parallel",)),
        name="inproj",
    )(x2, w_bf16, wkt)


def _rwkv_kernel(p_ref, mu_ref, w0_ref, a0_ref, kk_ref, ka_ref, rk_ref, lng_ref, lnb_ref,
                 wup_ref, aup_ref, gup_ref, ones_ref, o_ref, carry_ref, ht_ref, *, nb):
    c = RWKV_CHUNK
    w = RWKV_WIDTH
    rows = nb * c
    t = pl.program_id(1)

    @pl.when(t == 0)
    def _():
        carry_ref[...] = jnp.zeros_like(carry_ref)
        ht_ref[...] = jnp.zeros_like(ht_ref)

    p = p_ref[...].reshape(rows, RWKV_COLS)
    row = lax.broadcasted_iota(jnp.int32, (rows, 1), 0)
    prev = pltpu.roll(p, 1, axis=0)
    for n in range(nb):
        prev = jnp.where(row == n * c, carry_ref[n:n + 1, :], prev)
    for n in range(nb):
        carry_ref[n:n + 1, :] = p_ref[n, c - 1:c, :]
    ps = p + mu_ref[...] * (prev - p)

    r = ps[:, 0:w]
    k = ps[:, w:2 * w]
    v = ps[:, 2 * w:3 * w]
    lora_in = ps[:, 3 * w:3 * w + LANES]
    dg = ps[:, 3 * w + LANES:3 * w + 2 * LANES]

    w_log = -_softplus(-(w0_ref[...] + _dot(jnp.tanh(lora_in).astype(BF16), wup_ref[...]))) - 0.5
    logw = -jnp.exp(w_log)
    a = _sigmoid(a0_ref[...] + _dot(lora_in.astype(BF16), aup_ref[...]))
    g = _dot(_sigmoid(dg).astype(BF16), gup_ref[...])

    ones = ones_ref[...]

    def seg_sum(x):
        return jnp.concatenate([_dot(x[:, i:i + LANES].astype(BF16), ones) for i in range(0, w, LANES)], axis=1)

    kkr = k * kk_ref[...]
    kk = kkr / jnp.maximum(jnp.sqrt(seg_sum(kkr * kkr)), 1e-12)
    k2 = k * (1.0 + (a - 1.0) * ka_ref[...])
    b = kk * a

    ri = lax.broadcasted_iota(jnp.int32, (rows, rows), 0)
    ci = lax.broadcasted_iota(jnp.int32, (rows, rows), 1)
    tri = jnp.where(((ri // c) == (ci // c)) & (ci <= ri), 1.0, 0.0).astype(BF16)
    lw_hi = logw.astype(BF16)
    lw_lo = (logw - lw_hi.astype(F32)).astype(BF16)
    cum = _dot(tri, lw_hi) + _dot(tri, lw_lo)

    lane = lax.broadcasted_iota(jnp.int32, (1, LANES), 1)
    first = lane < HEAD_DIM

    def stack(x):
        return jnp.concatenate([jnp.where(first, x, 0.0), jnp.where(first, 0.0, x)], axis=0)

    ri2 = lax.broadcasted_iota(jnp.int32, (2 * c, 2 * c), 0)
    ci2 = lax.broadcasted_iota(jnp.int32, (2 * c, 2 * c), 1)
    same = (ri2 // c) == (ci2 // c)
    strict = same & (ci2 < ri2)
    incl = same & (ci2 <= ri2)
    eye = jnp.where(ri2 == ci2, 1.0, 0.0)

    def dotb(x, y):
        return _dot(x.astype(BF16), y.astype(BF16))

    cum_tot = jnp.concatenate(
        [jnp.broadcast_to(cum[n * c + c - 1:n * c + c, :], (c, w)) for n in range(nb)], axis=0)
    rt = r * jnp.exp(cum)
    at = -kk * jnp.exp(cum - logw)
    e_neg = jnp.exp(-cum)
    bt = b * e_neg
    kt = k2 * e_neg
    e_rem = jnp.exp(cum_tot - cum)
    bh = b * e_rem
    kh = k2 * e_rem
    e_tot = jnp.exp(cum_tot)

    chains = [(n, j) for n in range(nb) for j in range(w // LANES)]

    def cut(x, n, j):
        return x[n * c:(n + 1) * c, j * LANES:(j + 1) * LANES]

    ar = [jnp.concatenate([stack(cut(at, n, j)), stack(cut(rt, n, j))], axis=0).astype(BF16)
          for n, j in chains]
    bk = [jnp.concatenate([stack(cut(bt, n, j)), stack(cut(kt, n, j))], axis=0).astype(BF16)
          for n, j in chains]
    gram = [_dot_nt(x, y) for x, y in zip(ar, bk)]
    ht = [ht_ref[n, j] for n, j in chains]
    xh = [_dot_nt(x, h.astype(BF16)) for x, h in zip(ar, ht)]
    v_b = [stack(cut(v, n, j)).astype(BF16) for n, j in chains]
    a_ab = [jnp.where(strict, gm[0:2 * c, 0:2 * c], 0.0) for gm in gram]
    a_ak = [jnp.where(strict, gm[0:2 * c, 2 * c:4 * c], 0.0).astype(BF16) for gm in gram]
    a_r = [jnp.concatenate([jnp.where(incl, gm[2 * c:4 * c, 0:2 * c], 0.0),
                            jnp.where(incl, gm[2 * c:4 * c, 2 * c:4 * c], 0.0)], axis=1).astype(BF16)
           for gm in gram]
    rhs_u = [x[0:2 * c] + _dot(ak, vb) for x, ak, vb in zip(xh, a_ak, v_b)]

    pw = [x.astype(BF16) for x in a_ab]
    inv = [eye + x for x in a_ab]
    for _ in range(5):
        pw_f = [_dot(x, x) for x in pw]
        pw = [x.astype(BF16) for x in pw_f]
        inv = [x + _dot(x.astype(BF16), y) for x, y in zip(inv, pw)]

    u_d = [dotb(x, y) for x, y in zip(inv, rhs_u)]
    uv = [jnp.concatenate([u.astype(BF16), vb], axis=0) for u, vb in zip(u_d, v_b)]
    o_d = [x[2 * c:4 * c] + _dot(a, y) for x, a, y in zip(xh, a_r, uv)]
    bkh = [jnp.concatenate([stack(cut(bh, n, j)), stack(cut(kh, n, j))], axis=0).astype(BF16)
           for n, j in chains]
    upd = [_dot_tn(y, z) for y, z in zip(uv, bkh)]
    for (n, j), h, u in zip(chains, ht, upd):
        ht_ref[n, j] = h * e_tot[n * c:n * c + 1, j * LANES:(j + 1) * LANES] + u
    o_pair = [x[0:c] + x[c:2 * c] for x in o_d]
    npairs = w // LANES
    o = jnp.concatenate([jnp.concatenate(o_pair[n * npairs:(n + 1) * npairs], axis=1)
                         for n in range(nb)], axis=0)
    inv_n = 1.0 / HEAD_DIM
    d = o - seg_sum(o) * inv_n
    var = seg_sum(d * d) * inv_n
    on = d * lax.rsqrt(var + RWKV_GN_EPS) * lng_ref[...] + lnb_ref[...]
    bonus = seg_sum(r * k2 * rk_ref[...]) * v
    o_ref[...] = ((on + bonus) * g).reshape(nb, c, w)


def _rwkv(p_r, mu, w0, wup, a0, aup, gup, k_k, k_a, r_k, ln_g, ln_b, nb=4):
    bsz, seq, _ = p_r.shape
    nb = min(nb, bsz)
    c = RWKV_CHUNK
    w = RWKV_WIDTH
    zeros = jnp.zeros((DECAY_RANK, w), F32)
    wup_pad = jnp.concatenate([wup, zeros], axis=0).astype(BF16)
    aup_pad = jnp.concatenate([zeros, aup], axis=0).astype(BF16)
    head = jnp.arange(LANES) // HEAD_DIM
    ones = (head[:, None] == head[None, :]).astype(BF16)
    vec = lambda x: x.reshape(1, -1)
    full = lambda shape: pl.BlockSpec(shape, lambda b, t: (0,) * len(shape))
    return pl.pallas_call(
        functools.partial(_rwkv_kernel, nb=nb),
        grid=(bsz // nb, seq // c),
        in_specs=[pl.BlockSpec((nb, c, RWKV_COLS), lambda b, t: (b, t, 0)),
                  full((1, RWKV_COLS))] + [full((1, w))] * 7
                 + [full((LANES, w)), full((LANES, w)), full((GATE_RANK, w)), full((LANES, LANES))],
        out_specs=pl.BlockSpec((nb, c, w), lambda b, t: (b, t, 0)),
        out_shape=jax.ShapeDtypeStruct((bsz, seq, w), F32),
        scratch_shapes=[pltpu.VMEM((8, RWKV_COLS), F32),
                        pltpu.VMEM((nb, w // LANES, LANES, LANES), F32)],
        compiler_params=_params(("parallel", "arbitrary")),
        name="rwkv7",
    )(p_r, vec(mu), vec(w0), vec(a0), vec(k_k), vec(k_a), vec(r_k), vec(ln_g), vec(ln_b),
      wup_pad, aup_pad, gup.astype(BF16), ones)


def _lru_kernel(p_ref, cw_ref, cb_ref, wax_ref, bax_ref, lam_ref, ng_ref, ones_ref, o_ref,
                xs_ref, h_ref):
    tb = p_ref.shape[1]
    w = LRU_WIDTH
    hist = 8
    t = pl.program_id(1)

    @pl.when(t == 0)
    def _():
        xs_ref[0:hist, :] = jnp.zeros((hist, w), F32)
        h_ref[...] = jnp.zeros_like(h_ref)

    @pl.when(t > 0)
    def _():
        xs_ref[0:hist, :] = xs_ref[tb:tb + hist, :]

    gate_in = p_ref[0, :, 0:w]
    xin = p_ref[0, :, w:2 * w]
    xs_ref[hist:hist + tb, :] = xin
    xc = cb_ref[...] + cw_ref[CONV_WIDTH - 1:CONV_WIDTH, :] * xin
    for j in range(1, CONV_WIDTH):
        xc = xc + cw_ref[CONV_WIDTH - 1 - j:CONV_WIDTH - j, :] * xs_ref[hist - j:hist - j + tb, :]

    gates = _dot(xc.astype(BF16), wax_ref[...]) + bax_ref[...]
    gate_a = _sigmoid(gates[:, 0:w])
    gate_x = _sigmoid(gates[:, w:2 * w])
    log_a = -LRU_C * gate_a * _softplus(-lam_ref[...])
    a = jnp.exp(log_a)
    bb = jnp.sqrt(1.0 - jnp.exp(2.0 * log_a)) * (gate_x * xc)

    row = lax.broadcasted_iota(jnp.int32, (tb, 1), 0)
    d = 1
    while d < tb:
        keep = row >= d
        a_s = jnp.where(keep, pltpu.roll(a, d, axis=0), 1.0)
        b_s = jnp.where(keep, pltpu.roll(bb, d, axis=0), 0.0)
        bb = a * b_s + bb
        a = a * a_s
        d *= 2
    h = bb + a * h_ref[0:1, :]
    h_ref[0:1, :] = h[tb - 1:tb, :]

    gelu = 0.5 * gate_in * (1.0 + jnp.tanh(0.7978845608028654 * (gate_in + 0.044715 * (gate_in * gate_in * gate_in))))
    y = gelu * h
    ms = _dot((y * y).astype(BF16), ones_ref[...]) * (1.0 / HEAD_DIM)
    o_ref[0] = y * lax.rsqrt(ms + RMS_EPS) * ng_ref[...]


def _block_diag(wb):
    n, d, _ = wb.shape
    eye = jnp.eye(n, dtype=wb.dtype)
    return jnp.einsum('gij,gh->gihj', wb, eye).reshape(n * d, n * d)


def _lru(p_l, conv_w, conv_b, wa, ba, wx, bx, lam, norm_g, tb=512):
    bsz, seq, _ = p_l.shape
    w = LRU_WIDTH
    tb = min(tb, seq)
    wax = jnp.concatenate([_block_diag(wa), _block_diag(wx)], axis=1).astype(BF16)
    bax = jnp.concatenate([ba, bx]).reshape(1, 2 * w)
    head = jnp.arange(w) // HEAD_DIM
    ones = (head[:, None] == head[None, :]).astype(BF16)
    full = lambda shape: pl.BlockSpec(shape, lambda b, t: (0,) * len(shape))
    return pl.pallas_call(
        _lru_kernel,
        grid=(bsz, seq // tb),
        in_specs=[pl.BlockSpec((1, tb, LRU_COLS), lambda b, t: (b, t, 0)),
                  full((CONV_WIDTH, w)), full((1, w)), full((w, 2 * w)), full((1, 2 * w)),
                  full((1, w)), full((1, w)), full((w, w))],
        out_specs=pl.BlockSpec((1, tb, w), lambda b, t: (b, t, 0)),
        out_shape=jax.ShapeDtypeStruct((bsz, seq, w), F32),
        scratch_shapes=[pltpu.VMEM((tb + 8, w), F32), pltpu.VMEM((8, w), F32)],
        compiler_params=_params(("parallel", "arbitrary")),
        name="rglru",
    )(p_l, conv_w, conv_b.reshape(1, w), wax, bax, lam.reshape(1, w), norm_g.reshape(1, w), ones)


def _sba_kernel(q_ref, kt_ref, v_ref, g_ref, cs_ref, ones_ref, o_ref, acc_ref, car_ref, *, tq, tk):
    qi = pl.program_id(2)
    nsub = tq // tk
    lane = lax.broadcasted_iota(jnp.int32, (1, LANES), 1)
    first = lane < HEAD_DIM
    q = q_ref[0]
    zero = jnp.zeros_like(q)
    qh = (jnp.where(first, q, zero), jnp.where(first, zero, q))
    cs = cs_ref[...]
    acc_ref[...] = jnp.zeros_like(acc_ref)
    car_ref[...] = jnp.zeros_like(car_ref)

    def tile(j, r0, masked):
        kt = kt_ref[j]
        vt = v_ref[0, pl.ds(pl.multiple_of(j * tk, tk), tk), :]
        if masked:
            ti = lax.broadcasted_iota(jnp.int32, (tq - r0, tk), 0)
            si = lax.broadcasted_iota(jnp.int32, (tq - r0, tk), 1)
            mask = si < ti
        for h in range(2):
            z = _dot(qh[h][r0:], kt)
            neg_abs = lax.bitcast_convert_type(lax.bitcast_convert_type(z, jnp.int32) | SIGN_BIT, F32)
            nk = jnp.maximum(z, 0.0) + jnp.log2(1.0 + jnp.exp2(neg_abs))
            if masked:
                nk = jnp.where(mask, nk, 0.0)
            after = _dot(nk.astype(BF16), cs)
            car = car_ref[h, r0:, :]
            attn = jnp.exp2((z - nk) + after + car)
            if masked:
                attn = jnp.where(mask, attn, 0.0)
            acc_ref[h, r0:, :] += _dot(attn.astype(BF16), vt)
            car_ref[h, r0:, :] = car + (after[:, 0:1] - nk[:, 0:1])

    for d in range(nsub - 1, -1, -1):
        tile(qi * nsub + d, d * tk, True)

    def alive():
        return (jnp.max(car_ref[...]) >= SBA_DEAD_LOG2).astype(jnp.int32)

    def cond(state):
        i, live = state
        return (i < qi * nsub) & (live > 0)

    def body(state):
        i, _ = state
        tile(qi * nsub - 1 - i, 0, False)
        return i + 1, alive()

    lax.while_loop(cond, body, (jnp.int32(0), alive()))
    o = jnp.where(first, acc_ref[0], acc_ref[1])
    ms = _dot((o * o).astype(BF16), ones_ref[...]) * (1.0 / HEAD_DIM)
    o_ref[0] = o * lax.rsqrt(ms + RMS_EPS) * g_ref[...]


def _sba(q, kt, v, norm_g, tq=512):
    bsz, seq, _ = q.shape
    tk = kt.shape[2]
    tq = min(tq, seq)
    pairs = SBA_WIDTH // LANES
    ji = jnp.arange(tk)
    cs = -(ji[:, None] > ji[None, :]).astype(BF16)
    head = jnp.arange(LANES) // HEAD_DIM
    ones = (head[:, None] == head[None, :]).astype(BF16)
    return pl.pallas_call(
        functools.partial(_sba_kernel, tq=tq, tk=tk),
        grid=(bsz, pairs, seq // tq),
        in_specs=[pl.BlockSpec((1, tq, LANES), lambda b, h, i: (b, i, h)),
                  pl.BlockSpec((seq // tk, LANES, tk), lambda b, h, i: (b, h, 0)),
                  pl.BlockSpec((1, seq, LANES), lambda b, h, i: (b, 0, h)),
                  pl.BlockSpec((1, LANES), lambda b, h, i: (0, h)),
                  pl.BlockSpec((tk, tk), lambda b, h, i: (0, 0)),
                  pl.BlockSpec((LANES, LANES), lambda b, h, i: (0, 0))],
        out_specs=pl.BlockSpec((1, tq, LANES), lambda b, h, i: (b, i, h)),
        out_shape=jax.ShapeDtypeStruct((bsz, seq, SBA_WIDTH), F32),
        scratch_shapes=[pltpu.VMEM((2, tq, LANES), F32), pltpu.VMEM((2, tq, 1), F32)],
        compiler_params=_params(("parallel", "parallel", "arbitrary")),
        name="stickbreak",
    )(q, kt, v, norm_g.reshape(1, SBA_WIDTH), cs, ones)


def _layer_norm(h, g, b):
    mu = jnp.mean(h, axis=-1, keepdims=True)
    d = h - mu
    var = jnp.mean(d * d, axis=-1, keepdims=True)
    return d * lax.rsqrt(var + LN_EPS) * g + b


def _outproj_kernel(yr_ref, yl_ref, ys_ref, x_ref, w_ref, g_ref, b_ref, rw_ref, rb_ref,
                    x1_ref, g4_ref, grp_ref, cnt_ref):
    r0, r1 = RWKV_WIDTH, RWKV_WIDTH + LRU_WIDTH
    mix = (_dot(yr_ref[...].astype(BF16), w_ref[0:r0, :])
           + _dot(yl_ref[...].astype(BF16), w_ref[r0:r1, :])
           + _dot(ys_ref[...].astype(BF16), w_ref[r1:, :]))
    x1 = _layer_norm(ALPHA * x_ref[...] + mix, g_ref[...], b_ref[...])
    x1_ref[...] = x1

    logits = _dot3(rw_ref[...], x1, dot=_dot_nt) + rb_ref[...]
    mx = jnp.max(logits, axis=0, keepdims=True)
    ex = jnp.exp(logits - mx)
    scores = ex / jnp.sum(ex, axis=0, keepdims=True)
    s = [scores[e:e + 1, :] for e in range(N_EXPERTS)]
    top2 = []
    gsum = []
    for grp in range(N_GROUPS):
        mem = list(range(grp * EXPERTS_PER_GROUP, (grp + 1) * EXPERTS_PER_GROUP))
        tot = None
        for e in mem:
            rank = None
            for o in mem:
                if o == e:
                    continue
                ahead = (s[o] > s[e]) | ((s[o] == s[e]) if o < e else False)
                ahead = jnp.where(ahead, 1.0, 0.0)
                rank = ahead if rank is None else rank + ahead
            sel = rank < 1.5
            top2.append(sel)
            part = jnp.where(sel, s[e], 0.0)
            tot = part if tot is None else tot + part
        gsum.append(tot)
    best = gsum[0]
    best_g = jnp.zeros_like(best)
    for grp in range(1, N_GROUPS):
        upd = gsum[grp] > best
        best = jnp.where(upd, gsum[grp], best)
        best_g = jnp.where(upd, float(grp), best_g)
    rows = []
    for k in range(EXPERTS_PER_GROUP):
        acc = None
        for grp in range(N_GROUPS):
            e = grp * EXPERTS_PER_GROUP + k
            part = jnp.where((best_g == float(grp)) & top2[e], s[e] / gsum[grp], 0.0)
            acc = part if acc is None else acc + part
        rows.append(acc)
    g4_ref[...] = jnp.concatenate(rows, axis=0)
    grp_ref[...] = best_g.astype(jnp.int32)
    lane = lax.broadcasted_iota(jnp.int32, (1, LANES), 1)
    cnt = jnp.zeros((1, LANES), F32)
    for grp in range(N_GROUPS):
        n_grp = jnp.sum(jnp.where(best_g == float(grp), 1.0, 0.0), axis=1, keepdims=True)
        cnt = cnt + jnp.where(lane == grp, n_grp, 0.0)
    cnt_ref[0] = cnt.astype(jnp.int32)


def _outproj(y_r, y_l, y_s, x2, w_bf16, ln_g, ln_b, router_wt, router_b, tm=ROUTE_TILE):
    t, d = x2.shape
    tm = min(tm, t)
    row = lambda width: pl.BlockSpec((tm, width), lambda i: (i, 0))
    full = lambda shape: pl.BlockSpec(shape, lambda i: (0,) * len(shape))
    return pl.pallas_call(
        _outproj_kernel,
        grid=(t // tm,),
        in_specs=[row(RWKV_WIDTH), row(LRU_WIDTH), row(SBA_WIDTH), row(d), full((d, d)),
                  full((1, d)), full((1, d)), full((N_EXPERTS, d)), full((N_EXPERTS, 1))],
        out_specs=[row(d), pl.BlockSpec((EXPERTS_PER_GROUP, tm), lambda i: (0, i)),
                   pl.BlockSpec((1, tm), lambda i: (0, i)),
                   pl.BlockSpec((1, 1, LANES), lambda i: (i, 0, 0))],
        out_shape=[jax.ShapeDtypeStruct((t, d), F32),
                   jax.ShapeDtypeStruct((EXPERTS_PER_GROUP, t), F32),
                   jax.ShapeDtypeStruct((1, t), jnp.int32),
                   jax.ShapeDtypeStruct((t // tm, 1, LANES), jnp.int32)],
        compiler_params=_params(("parallel",)),
        name="outproj_ln_router",
    )(y_r, y_l, y_s, x2, w_bf16, ln_g.reshape(1, d), ln_b.reshape(1, d), router_wt,
      router_b.reshape(N_EXPERTS, 1))


def _moe_kernel(cnt_ref, x_ref, grp_ref, g4_ref, wg_ref, wu_ref, wd_ref, g_ref, b_ref, o_ref,
                xs_ref, gs_ref, ys_ref, dest_ref, *, tm, rb, nsub):
    i = pl.program_id(0)
    e = pl.program_id(1)
    cap = xs_ref.shape[0]
    chunks = tm // LANES
    n = [sum(cnt_ref[i * nsub + s, grp] for s in range(nsub)) for grp in range(N_GROUPS)]
    start = [0]
    for grp in range(1, N_GROUPS):
        start.append(start[-1] + n[grp - 1])

    @pl.when(e == 0)
    def _():
        grp = grp_ref[...]
        onehot = jnp.concatenate([jnp.where(grp == k, 1.0, 0.0) for k in range(N_GROUPS)], axis=0)
        li = lax.broadcasted_iota(jnp.int32, (LANES, LANES), 0)
        lj = lax.broadcasted_iota(jnp.int32, (LANES, LANES), 1)
        before = jnp.where(li < lj, 1.0, 0.0).astype(BF16)
        within = _dot(onehot.astype(BF16), before)
        tot = jnp.broadcast_to(jnp.sum(onehot, axis=1, keepdims=True), onehot.shape)
        ri = lax.broadcasted_iota(jnp.int32, (N_GROUPS * chunks, N_GROUPS * chunks), 0)
        ci = lax.broadcasted_iota(jnp.int32, (N_GROUPS * chunks, N_GROUPS * chunks), 1)
        earlier = jnp.where(((ri // chunks) == (ci // chunks)) & (ci < ri), 1.0, 0.0).astype(BF16)
        rank = within + _dot(earlier, tot.astype(BF16))
        dest = jnp.zeros((chunks, LANES), F32)
        for k in range(N_GROUPS):
            dest = jnp.where(grp == k, rank[k * chunks:(k + 1) * chunks] + jnp.asarray(start[k]).astype(F32), dest)
        dest_ref[...] = dest
        dest_i = dest.astype(jnp.int32)
        r_iota = lax.broadcasted_iota(jnp.int32, (tm, LANES), 0)
        place = jnp.concatenate(
            [jnp.where(r_iota == dest_i[c:c + 1, :], 1.0, 0.0).astype(BF16) for c in range(chunks)],
            axis=1)
        xs_ref[0:tm, :] = _dot(place, x_ref[...].astype(BF16)).astype(BF16)
        xs_ref[tm:cap, :] = jnp.zeros((cap - tm, xs_ref.shape[1]), BF16)
        pieces = [p.astype(F32) for p in _split3(g4_ref[...])]
        gates_t = jnp.transpose(jnp.concatenate(
            pieces + [jnp.zeros((LANES - 3 * EXPERTS_PER_GROUP, tm), F32)], axis=0))
        gs_ref[0:tm, :] = _dot(place, gates_t.astype(BF16))
        gs_ref[tm:cap, :] = jnp.zeros((cap - tm, LANES), F32)
        ys_ref[...] = jnp.zeros_like(ys_ref)

    grp_e = e // EXPERTS_PER_GROUP
    k_e = e % EXPERTS_PER_GROUP
    n_e = n[0]
    s_e = start[0]
    for grp in range(1, N_GROUPS):
        n_e = jnp.where(grp_e == grp, n[grp], n_e)
        s_e = jnp.where(grp_e == grp, start[grp], s_e)
    lane = lax.broadcasted_iota(jnp.int32, (1, LANES), 1)
    mine = ((lane % EXPERTS_PER_GROUP) == k_e) & (lane < 3 * EXPERTS_PER_GROUP)
    s_al = s_e // MOE_ALIGN * MOE_ALIGN

    def block(b, carry):
        r0 = pl.multiple_of(s_al + b * rb, MOE_ALIGN)
        xb = xs_ref[pl.ds(r0, rb), :]
        gate = jnp.sum(jnp.where(mine, gs_ref[pl.ds(r0, rb), :], 0.0), axis=1, keepdims=True)
        rid = r0 + lax.broadcasted_iota(jnp.int32, (rb, 1), 0)
        gate = jnp.where((rid >= s_e) & (rid < s_e + n_e), gate, 0.0)
        hg = _dot(xb, wg_ref[0])
        hu = _dot(xb, wu_ref[0])
        h = hg * _sigmoid(hg) * hu * gate
        ys_ref[pl.ds(r0, rb), :] += _dot(h.astype(BF16), wd_ref[0])
        return carry

    lax.fori_loop(0, jnp.where(n_e > 0, (s_e + n_e - s_al + rb - 1) // rb, 0), block, 0)

    @pl.when(e == N_EXPERTS - 1)
    def _():
        dest = dest_ref[...]
        dest_t = jnp.transpose(jnp.concatenate([dest, jnp.zeros((LANES - chunks, LANES), F32)], axis=0))
        dest_t = dest_t.astype(jnp.int32)
        c_iota = lax.broadcasted_iota(jnp.int32, (LANES, tm), 1)
        unplace = jnp.concatenate(
            [jnp.where(c_iota == dest_t[:, c:c + 1], 1.0, 0.0).astype(BF16) for c in range(chunks)],
            axis=0)
        moe = _dot(unplace, ys_ref[0:tm, :].astype(BF16))
        o_ref[...] = _layer_norm(ALPHA * x_ref[...] + moe, g_ref[...], b_ref[...])


def _moe(x1, g4, grp, cnt, wg, wu, wd, ln_g, ln_b, tm=MOE_TILE, rb=MOE_ROWS):
    t, d = x1.shape
    de = wg.shape[2]
    tm = min(tm, t)
    nsub = cnt.shape[0] * tm // t
    cap = tm + rb
    grid_spec = pltpu.PrefetchScalarGridSpec(
        num_scalar_prefetch=1,
        grid=(t // tm, N_EXPERTS),
        in_specs=[pl.BlockSpec((tm, d), lambda i, e, c: (i, 0)),
                  pl.BlockSpec((tm // LANES, LANES), lambda i, e, c: (i, 0)),
                  pl.BlockSpec((EXPERTS_PER_GROUP, tm), lambda i, e, c: (0, i)),
                  pl.BlockSpec((1, d, de), lambda i, e, c: (e, 0, 0)),
                  pl.BlockSpec((1, d, de), lambda i, e, c: (e, 0, 0)),
                  pl.BlockSpec((1, de, d), lambda i, e, c: (e, 0, 0)),
                  pl.BlockSpec((1, d), lambda i, e, c: (0, 0)),
                  pl.BlockSpec((1, d), lambda i, e, c: (0, 0))],
        out_specs=pl.BlockSpec((tm, d), lambda i, e, c: (i, 0)),
        scratch_shapes=[pltpu.VMEM((cap, d), BF16), pltpu.VMEM((cap, LANES), F32),
                        pltpu.VMEM((cap, d), F32), pltpu.VMEM((tm // LANES, LANES), F32)])
    return pl.pallas_call(
        functools.partial(_moe_kernel, tm=tm, rb=rb, nsub=nsub),
        grid_spec=grid_spec,
        out_shape=jax.ShapeDtypeStruct((t, d), F32),
        compiler_params=_params(("parallel", "arbitrary")),
        name="moe_ln",
    )(cnt.reshape(cnt.shape[0], LANES)[:, :N_GROUPS], x1, grp.reshape(t // LANES, LANES), g4,
      wg, wu, wd, ln_g.reshape(1, d), ln_b.reshape(1, d))


def kernel(x, w_in, shift_mu, rwkv_w0, rwkv_w_up, rwkv_a0, rwkv_a_up, rwkv_g_up, rwkv_k_k, rwkv_k_a, rwkv_r_k, rwkv_ln_g, rwkv_ln_b, lru_conv_w, lru_conv_b, lru_wa, lru_ba, lru_wx, lru_bx, lru_lambda, lru_norm_g, sba_norm_g, w_out, ln1_g, ln1_b, ln2_g, ln2_b, router_w, router_b, exp_w_gate, exp_w_up, exp_w_down):
    bsz, seq, d = x.shape
    t = bsz * seq
    x2 = x.reshape(t, d)
    router_wt = router_w.T
    for l in range(w_in.shape[0]):
        p_r, p_l, p_q, p_kt, p_v = _inproj(x2, w_in[l])
        y_r = _rwkv(p_r.reshape(bsz, seq, RWKV_COLS), shift_mu[l], rwkv_w0[l], rwkv_w_up[l],
                    rwkv_a0[l], rwkv_a_up[l], rwkv_g_up[l], rwkv_k_k[l], rwkv_k_a[l],
                    rwkv_r_k[l], rwkv_ln_g[l], rwkv_ln_b[l])
        y_l = _lru(p_l.reshape(bsz, seq, LRU_COLS), lru_conv_w[l], lru_conv_b[l], lru_wa[l],
                   lru_ba[l], lru_wx[l], lru_bx[l], lru_lambda[l], lru_norm_g[l])
        y_s = _sba(p_q.reshape(bsz, seq, SBA_WIDTH), p_kt, p_v.reshape(bsz, seq, SBA_WIDTH), sba_norm_g[l])
        x1, g4, grp, cnt = _outproj(y_r.reshape(t, RWKV_WIDTH), y_l.reshape(t, LRU_WIDTH),
                                    y_s.reshape(t, SBA_WIDTH), x2, w_out[l].astype(BF16),
                                    ln1_g[l], ln1_b[l], router_wt, router_b)
        x2 = _moe(x1, g4, grp, cnt, exp_w_gate[l].astype(BF16), exp_w_up[l].astype(BF16),
                  exp_w_down[l].astype(BF16), ln2_g[l], ln2_b[l])
    return x2.reshape(bsz, seq, d)
```

```python
import functools

import jax
import jax.numpy as jnp
from jax import lax
from jax.experimental import pallas as pl
from jax.experimental.pallas import tpu as pltpu

F32 = jnp.float32
BF16 = jnp.bfloat16

HEAD_DIM = 64
RWKV_WIDTH = 384
LRU_WIDTH = 256
SBA_WIDTH = 384
DECAY_RANK = 64
ICLR_RANK = 64
GATE_RANK = 128
RWKV_COLS = 3 * RWKV_WIDTH + DECAY_RANK + ICLR_RANK + GATE_RANK
LRU_COLS = 2 * LRU_WIDTH
SBA_COLS = 3 * SBA_WIDTH
RWKV_GN_EPS = 64e-5
CONV_WIDTH = 4
LRU_C = 8.0
N_EXPERTS = 16
N_GROUPS = 4
EXPERTS_PER_GROUP = 4
DEPTH = 4
ALPHA = (2.0 * DEPTH) ** 0.25
LN_EPS = 1e-5
RMS_EPS = 1e-6
LOG2E = 1.4426950408889634
SIGN_BIT = -2147483648
SBA_DEAD_LOG2 = -160.0

LANES = 128
RWKV_CHUNK = 64
SBA_KEY_TILE = 256
ROUTE_TILE = 512
PROJ_CHUNK = 256
MOE_TILE = 1024
MOE_ROWS = 288
MOE_ALIGN = 16
MOE_EXPERTS_PER_STEP = 2
VMEM_LIMIT = 48 * 1024 * 1024


def _dot(a, b):
    return jnp.dot(a, b, preferred_element_type=F32)


def _dot_nt(a, b):
    return lax.dot_general(a, b, (((1,), (1,)), ((), ())), preferred_element_type=F32)


def _dot_tn(a, b):
    return lax.dot_general(a, b, (((0,), (0,)), ((), ())), preferred_element_type=F32)


def _split3(x):
    hi = x.astype(BF16)
    r1 = x - hi.astype(F32)
    mid = r1.astype(BF16)
    lo = (r1 - mid.astype(F32)).astype(BF16)
    return hi, mid, lo


def _dot_sel_rhs(x, sel):
    hi, mid, lo = _split3(x)
    return _dot(hi, sel) + _dot(mid, sel) + _dot(lo, sel)


def _dot_sel_lhs(sel, x):
    hi, mid, lo = _split3(x)
    return _dot(sel, hi) + _dot(sel, mid) + _dot(sel, lo)


def _dot3(a, b, dot=_dot):
    ah, am, al = _split3(a)
    bh, bm, bl = _split3(b)
    return (dot(ah, bh) + (dot(ah, bm) + dot(am, bh))
            + (dot(ah, bl) + dot(am, bm) + dot(al, bh)))


def _softplus(x):
    return jnp.maximum(x, 0.0) + jnp.log1p(jnp.exp(-jnp.abs(x)))


def _sigmoid(x):
    return 1.0 / (1.0 + jnp.exp(-x))


def _params(sem):
    return pltpu.CompilerParams(dimension_semantics=sem, vmem_limit_bytes=VMEM_LIMIT)


def _rglru_stages(p, first, cw_ref, cb_ref, wax_ref, bax_ref, lam_ref, ng_ref, ones_ref, o_ref, xs_ref, h_ref):
    tb = p.shape[0]
    w = LRU_WIDTH
    hist = 8
    xs_ref[0:hist, :] = jnp.where(first, 0.0, xs_ref[tb:tb + hist, :])
    gate_in = p[:, 0:w]
    xin = p[:, w:2 * w]
    xs_ref[hist:hist + tb, :] = xin
    xc = cb_ref[...] + cw_ref[CONV_WIDTH - 1:CONV_WIDTH, :] * xin
    for j in range(1, CONV_WIDTH):
        xc = xc + cw_ref[CONV_WIDTH - 1 - j:CONV_WIDTH - j, :] * xs_ref[hist - j:hist - j + tb, :]
    gates = _dot(xc.astype(BF16), wax_ref[...]) + bax_ref[...]
    yield
    gate_a = _sigmoid(gates[:, 0:w])
    gate_x = _sigmoid(gates[:, w:2 * w])
    log_a = -LRU_C * gate_a * _softplus(-lam_ref[...])
    a = jnp.exp(log_a)
    bb = jnp.sqrt(1.0 - jnp.exp(2.0 * log_a)) * (gate_x * xc)
    yield
    row = lax.broadcasted_iota(jnp.int32, (tb, 1), 0)
    d = 1
    while d < tb:
        keep = row >= d
        a_s = jnp.where(keep, pltpu.roll(a, d, axis=0), 1.0)
        b_s = jnp.where(keep, pltpu.roll(bb, d, axis=0), 0.0)
        bb = a * b_s + bb
        a = a * a_s
        d *= 2
        yield
    h = bb + a * jnp.where(first, 0.0, h_ref[0:1, :])
    h_ref[0:1, :] = h[tb - 1:tb, :]
    gelu = 0.5 * gate_in * (1.0 + jnp.tanh(0.7978845608028654 * (gate_in + 0.044715 * (gate_in * gate_in * gate_in))))
    y = gelu * h
    ms = _dot((y * y).astype(BF16), ones_ref[...]) * (1.0 / HEAD_DIM)
    o_ref[...] = y * lax.rsqrt(ms + RMS_EPS) * ng_ref[...]


def _inproj_kernel(x_ref, w_ref, wkt_ref, cw_ref, cb_ref, wax_ref, bax_ref, lam_ref, ng_ref, ones_ref,
                   o_r, o_yl, o_q, o_kt, o_v, xs_ref, h_ref, *, tiles_per_seq):
    xb = x_ref[...].astype(BF16)
    first = pl.program_id(0) % tiles_per_seq == 0
    p_l = _dot(xb, w_ref[:, RWKV_COLS:RWKV_COLS + LRU_COLS])
    lru = _rglru_stages(p_l, first, cw_ref, cb_ref, wax_ref, bax_ref, lam_ref, ng_ref, ones_ref, o_yl,
                        xs_ref, h_ref)

    def project(o_ref, col, c0, c1):
        o_ref[:, c0:c1] = _dot(xb, w_ref[:, col + c0:col + c1]).astype(o_ref.dtype)

    tk = o_kt.shape[2]

    def project_keys(r0, r1):
        kt = _dot_nt(wkt_ref[r0:r1, :], xb)
        for i in range(o_kt.shape[0]):
            o_kt[i, r0:r1, :] = kt[:, i * tk:(i + 1) * tk].astype(o_kt.dtype)

    work = []
    col = 0
    for o_ref, skip in ((o_r, LRU_COLS), (o_q, SBA_WIDTH), (o_v, 0)):
        width = o_ref.shape[1]
        for c0 in range(0, width, PROJ_CHUNK):
            work.append(functools.partial(project, o_ref, col, c0, min(c0 + PROJ_CHUNK, width)))
        col += width + skip
    for r0 in range(0, SBA_WIDTH, LANES):
        work.append(functools.partial(project_keys, r0, r0 + LANES))
    for step in work:
        step()
        next(lru, None)
    for _ in lru:
        pass


def _block_diag(wb):
    n, d, _ = wb.shape
    eye = jnp.eye(n, dtype=wb.dtype)
    return jnp.einsum('gij,gh->gihj', wb, eye).reshape(n * d, n * d)


def _inproj(x2, seq, w, conv_w, conv_b, wa, ba, wx, bx, lam, norm_g, tm=512, tk=SBA_KEY_TILE):
    t, d = x2.shape
    n = w.shape[1]
    tm = min(tm, seq)
    lw = LRU_WIDTH
    q0 = RWKV_COLS + LRU_COLS
    k0 = q0 + SBA_WIDTH
    col = jnp.arange(n)
    w_bf16 = (w * jnp.where((col >= q0) & (col < k0), LOG2E * HEAD_DIM ** -0.5, 1.0)).astype(BF16)
    wkt = w_bf16[:, k0:k0 + SBA_WIDTH].T
    wax = jnp.concatenate([_block_diag(wa), _block_diag(wx)], axis=1).astype(BF16)
    bax = jnp.concatenate([ba, bx]).reshape(1, 2 * lw)
    head = jnp.arange(lw) // HEAD_DIM
    ones = (head[:, None] == head[None, :]).astype(BF16)
    full = lambda shape: pl.BlockSpec(shape, lambda i: (0,) * len(shape))
    return pl.pallas_call(
        functools.partial(_inproj_kernel, tiles_per_seq=seq // tm),
        grid=(t // tm,),
        in_specs=[pl.BlockSpec((tm, d), lambda i: (i, 0)), full((d, n)), full((SBA_WIDTH, d)),
                  full((CONV_WIDTH, lw)), full((1, lw)), full((lw, 2 * lw)), full((1, 2 * lw)),
                  full((1, lw)), full((1, lw)), full((lw, lw))],
        out_specs=[pl.BlockSpec((tm, RWKV_COLS), lambda i: (i, 0)),
                   pl.BlockSpec((tm, lw), lambda i: (i, 0)),
                   pl.BlockSpec((tm, SBA_WIDTH), lambda i: (i, 0)),
                   pl.BlockSpec((tm // tk, SBA_WIDTH, tk), lambda i: (i, 0, 0)),
                   pl.BlockSpec((tm, SBA_WIDTH), lambda i: (i, 0))],
        out_shape=[jax.ShapeDtypeStruct((t, RWKV_COLS), F32),
                   jax.ShapeDtypeStruct((t, lw), F32),
                   jax.ShapeDtypeStruct((t, SBA_WIDTH), BF16),
                   jax.ShapeDtypeStruct((t // tk, SBA_WIDTH, tk), BF16),
                   jax.ShapeDtypeStruct((t, SBA_WIDTH), BF16)],
        scratch_shapes=[pltpu.VMEM((tm + 8, lw), F32), pltpu.VMEM((8, lw), F32)],
        compiler_params=_params(("arbitrary",)),
        name="inproj_rglru",
    )(x2, w_bf16, wkt, conv_w, conv_b.reshape(1, lw), wax, bax, lam.reshape(1, lw), norm_g.reshape(1, lw), ones)


def _rwkv_kernel(p_ref, mu_ref, w0_ref, a0_ref, kk_ref, ka_ref, rk_ref, lng_ref, lnb_ref,
                 wup_ref, aup_ref, gup_ref, ones_ref, o_ref, carry_ref, ht_ref, *, nb):
    c = RWKV_CHUNK
    w = RWKV_WIDTH
    rows = nb * c
    t = pl.program_id(1)

    @pl.when(t == 0)
    def _():
        carry_ref[...] = jnp.zeros_like(carry_ref)
        ht_ref[...] = jnp.zeros_like(ht_ref)

    p = p_ref[...].reshape(rows, RWKV_COLS)
    row = lax.broadcasted_iota(jnp.int32, (rows, 1), 0)
    prev = pltpu.roll(p, 1, axis=0)
    for n in range(nb):
        prev = jnp.where(row == n * c, carry_ref[n:n + 1, :], prev)
    for n in range(nb):
        carry_ref[n:n + 1, :] = p_ref[n, c - 1:c, :]
    ps = p + mu_ref[...] * (prev - p)

    r = ps[:, 0:w]
    k = ps[:, w:2 * w]
    v = ps[:, 2 * w:3 * w]
    lora_in = ps[:, 3 * w:3 * w + LANES]
    dg = ps[:, 3 * w + LANES:3 * w + 2 * LANES]

    w_log = -_softplus(-(w0_ref[...] + _dot(jnp.tanh(lora_in).astype(BF16), wup_ref[...]))) - 0.5
    logw = -jnp.exp(w_log)
    a = _sigmoid(a0_ref[...] + _dot(lora_in.astype(BF16), aup_ref[...]))
    g = _dot(_sigmoid(dg).astype(BF16), gup_ref[...])

    ones = ones_ref[...]

    def seg_sum(x):
        return jnp.concatenate([_dot(x[:, i:i + LANES].astype(BF16), ones) for i in range(0, w, LANES)], axis=1)

    kkr = k * kk_ref[...]
    kk = kkr / jnp.maximum(jnp.sqrt(seg_sum(kkr * kkr)), 1e-12)
    k2 = k * (1.0 + (a - 1.0) * ka_ref[...])
    b = kk * a

    ri = lax.broadcasted_iota(jnp.int32, (rows, rows), 0)
    ci = lax.broadcasted_iota(jnp.int32, (rows, rows), 1)
    tri = jnp.where(((ri // c) == (ci // c)) & (ci <= ri), 1.0, 0.0).astype(BF16)
    lw_hi = logw.astype(BF16)
    lw_lo = (logw - lw_hi.astype(F32)).astype(BF16)
    cum = _dot(tri, lw_hi) + _dot(tri, lw_lo)

    lane = lax.broadcasted_iota(jnp.int32, (1, LANES), 1)
    first = lane < HEAD_DIM

    def stack(x):
        return jnp.concatenate([jnp.where(first, x, 0.0), jnp.where(first, 0.0, x)], axis=0)

    ri2 = lax.broadcasted_iota(jnp.int32, (2 * c, 2 * c), 0)
    ci2 = lax.broadcasted_iota(jnp.int32, (2 * c, 2 * c), 1)
    same = (ri2 // c) == (ci2 // c)
    strict = same & (ci2 < ri2)
    incl = same & (ci2 <= ri2)
    eye = jnp.where(ri2 == ci2, 1.0, 0.0)

    def dotb(x, y):
        return _dot(x.astype(BF16), y.astype(BF16))

    cum_tot = jnp.concatenate(
        [jnp.broadcast_to(cum[n * c + c - 1:n * c + c, :], (c, w)) for n in range(nb)], axis=0)
    rt = r * jnp.exp(cum)
    at = -kk * jnp.exp(cum - logw)
    e_neg = jnp.exp(-cum)
    bt = b * e_neg
    kt = k2 * e_neg
    e_rem = jnp.exp(cum_tot - cum)
    bh = b * e_rem
    kh = k2 * e_rem
    e_tot = jnp.exp(cum_tot)

    chains = [(n, j) for n in range(nb) for j in range(w // LANES)]

    def cut(x, n, j):
        return x[n * c:(n + 1) * c, j * LANES:(j + 1) * LANES]

    ar = [jnp.concatenate([stack(cut(at, n, j)), stack(cut(rt, n, j))], axis=0).astype(BF16)
          for n, j in chains]
    bk = [jnp.concatenate([stack(cut(bt, n, j)), stack(cut(kt, n, j))], axis=0).astype(BF16)
          for n, j in chains]
    gram = [_dot_nt(x, y) for x, y in zip(ar, bk)]
    ht = [ht_ref[n, j] for n, j in chains]
    xh = [_dot_nt(x, h.astype(BF16)) for x, h in zip(ar, ht)]
    v_b = [stack(cut(v, n, j)).astype(BF16) for n, j in chains]
    a_ab = [jnp.where(strict, gm[0:2 * c, 0:2 * c], 0.0) for gm in gram]
    a_ak = [jnp.where(strict, gm[0:2 * c, 2 * c:4 * c], 0.0).astype(BF16) for gm in gram]
    a_r = [jnp.concatenate([jnp.where(incl, gm[2 * c:4 * c, 0:2 * c], 0.0),
                            jnp.where(incl, gm[2 * c:4 * c, 2 * c:4 * c], 0.0)], axis=1).astype(BF16)
           for gm in gram]
    rhs_u = [x[0:2 * c] + _dot(ak, vb) for x, ak, vb in zip(xh, a_ak, v_b)]

    pw = [x.astype(BF16) for x in a_ab]
    inv = [eye + x for x in a_ab]
    for _ in range(5):
        pw_f = [_dot(x, x) for x in pw]
        pw = [x.astype(BF16) for x in pw_f]
        inv = [x + _dot(x.astype(BF16), y) for x, y in zip(inv, pw)]

    u_d = [dotb(x, y) for x, y in zip(inv, rhs_u)]
    uv = [jnp.concatenate([u.astype(BF16), vb], axis=0) for u, vb in zip(u_d, v_b)]
    o_d = [x[2 * c:4 * c] + _dot(a, y) for x, a, y in zip(xh, a_r, uv)]
    bkh = [jnp.concatenate([stack(cut(bh, n, j)), stack(cut(kh, n, j))], axis=0).astype(BF16)
           for n, j in chains]
    upd = [_dot_tn(y, z) for y, z in zip(uv, bkh)]
    for (n, j), h, u in zip(chains, ht, upd):
        ht_ref[n, j] = h * e_tot[n * c:n * c + 1, j * LANES:(j + 1) * LANES] + u
    o_pair = [x[0:c] + x[c:2 * c] for x in o_d]
    npairs = w // LANES
    o = jnp.concatenate([jnp.concatenate(o_pair[n * npairs:(n + 1) * npairs], axis=1)
                         for n in range(nb)], axis=0)
    inv_n = 1.0 / HEAD_DIM
    d = o - seg_sum(o) * inv_n
    var = seg_sum(d * d) * inv_n
    on = d * lax.rsqrt(var + RWKV_GN_EPS) * lng_ref[...] + lnb_ref[...]
    bonus = seg_sum(r * k2 * rk_ref[...]) * v
    o_ref[...] = ((on + bonus) * g).reshape(nb, c, w)


def _rwkv(p_r, mu, w0, wup, a0, aup, gup, k_k, k_a, r_k, ln_g, ln_b, nb=8):
    bsz, seq, _ = p_r.shape
    nb = min(nb, bsz)
    c = RWKV_CHUNK
    w = RWKV_WIDTH
    zeros = jnp.zeros((DECAY_RANK, w), F32)
    wup_pad = jnp.concatenate([wup, zeros], axis=0).astype(BF16)
    aup_pad = jnp.concatenate([zeros, aup], axis=0).astype(BF16)
    head = jnp.arange(LANES) // HEAD_DIM
    ones = (head[:, None] == head[None, :]).astype(BF16)
    vec = lambda x: x.reshape(1, -1)
    full = lambda shape: pl.BlockSpec(shape, lambda b, t: (0,) * len(shape))
    return pl.pallas_call(
        functools.partial(_rwkv_kernel, nb=nb),
        grid=(bsz // nb, seq // c),
        in_specs=[pl.BlockSpec((nb, c, RWKV_COLS), lambda b, t: (b, t, 0)),
                  full((1, RWKV_COLS))] + [full((1, w))] * 7
                 + [full((LANES, w)), full((LANES, w)), full((GATE_RANK, w)), full((LANES, LANES))],
        out_specs=pl.BlockSpec((nb, c, w), lambda b, t: (b, t, 0)),
        out_shape=jax.ShapeDtypeStruct((bsz, seq, w), F32),
        scratch_shapes=[pltpu.VMEM((8, RWKV_COLS), F32),
                        pltpu.VMEM((nb, w // LANES, LANES, LANES), F32)],
        compiler_params=_params(("parallel", "arbitrary")),
        name="rwkv7",
    )(p_r, vec(mu), vec(w0), vec(a0), vec(k_k), vec(k_a), vec(r_k), vec(ln_g), vec(ln_b),
      wup_pad, aup_pad, gup.astype(BF16), ones)


def _sba_kernel(q_ref, kt_ref, v_ref, g_ref, cs_ref, ones_ref, o_ref, acc_ref, car_ref, *, tq, tk):
    qi = pl.program_id(2)
    nsub = tq // tk
    lane = lax.broadcasted_iota(jnp.int32, (1, LANES), 1)
    first = lane < HEAD_DIM
    q = q_ref[0]
    zero = jnp.zeros_like(q)
    qh = (jnp.where(first, q, zero), jnp.where(first, zero, q))
    cs = cs_ref[...]
    acc_ref[...] = jnp.zeros_like(acc_ref)
    car_ref[...] = jnp.zeros_like(car_ref)

    def tile(j, r0, masked):
        kt = kt_ref[j]
        vt = v_ref[0, pl.ds(pl.multiple_of(j * tk, tk), tk), :]
        if masked:
            ti = lax.broadcasted_iota(jnp.int32, (tq - r0, tk), 0)
            si = lax.broadcasted_iota(jnp.int32, (tq - r0, tk), 1)
            mask = si < ti
        for h in range(2):
            z = _dot(qh[h][r0:], kt)
            neg_abs = lax.bitcast_convert_type(lax.bitcast_convert_type(z, jnp.int32) | SIGN_BIT, F32)
            nk = jnp.maximum(z, 0.0) + jnp.log2(1.0 + jnp.exp2(neg_abs))
            if masked:
                nk = jnp.where(mask, nk, 0.0)
            after = _dot(nk.astype(BF16), cs)
            car = car_ref[h, r0:, :]
            attn = jnp.exp2((z - nk) + after + car)
            if masked:
                attn = jnp.where(mask, attn, 0.0)
            acc_ref[h, r0:, :] += _dot(attn.astype(BF16), vt)
            car_ref[h, r0:, :] = car + (after[:, 0:1] - nk[:, 0:1])

    for d in range(nsub - 1, -1, -1):
        tile(qi * nsub + d, d * tk, True)

    def alive():
        return (jnp.max(car_ref[...]) >= SBA_DEAD_LOG2).astype(jnp.int32)

    def cond(state):
        i, live = state
        return (i < qi * nsub) & (live > 0)

    def body(state):
        i, _ = state
        tile(qi * nsub - 1 - i, 0, False)
        return i + 1, alive()

    lax.while_loop(cond, body, (jnp.int32(0), alive()))
    o = jnp.where(first, acc_ref[0], acc_ref[1])
    ms = _dot((o * o).astype(BF16), ones_ref[...]) * (1.0 / HEAD_DIM)
    o_ref[0] = o * lax.rsqrt(ms + RMS_EPS) * g_ref[...]


def _sba(q, kt, v, norm_g, tq=512):
    bsz, seq, _ = q.shape
    tk = kt.shape[2]
    tq = min(tq, seq)
    pairs = SBA_WIDTH // LANES
    ji = jnp.arange(tk)
    cs = -(ji[:, None] > ji[None, :]).astype(BF16)
    head = jnp.arange(LANES) // HEAD_DIM
    ones = (head[:, None] == head[None, :]).astype(BF16)
    return pl.pallas_call(
        functools.partial(_sba_kernel, tq=tq, tk=tk),
        grid=(bsz, pairs, seq // tq),
        in_specs=[pl.BlockSpec((1, tq, LANES), lambda b, h, i: (b, i, h)),
                  pl.BlockSpec((seq // tk, LANES, tk), lambda b, h, i: (b, h, 0)),
                  pl.BlockSpec((1, seq, LANES), lambda b, h, i: (b, 0, h)),
                  pl.BlockSpec((1, LANES), lambda b, h, i: (0, h)),
                  pl.BlockSpec((tk, tk), lambda b, h, i: (0, 0)),
                  pl.BlockSpec((LANES, LANES), lambda b, h, i: (0, 0))],
        out_specs=pl.BlockSpec((1, tq, LANES), lambda b, h, i: (b, i, h)),
        out_shape=jax.ShapeDtypeStruct((bsz, seq, SBA_WIDTH), F32),
        scratch_shapes=[pltpu.VMEM((2, tq, LANES), F32), pltpu.VMEM((2, tq, 1), F32)],
        compiler_params=_params(("parallel", "parallel", "arbitrary")),
        name="stickbreak",
    )(q, kt, v, norm_g.reshape(1, SBA_WIDTH), cs, ones)


def _layer_norm(h, g, b):
    mu = jnp.mean(h, axis=-1, keepdims=True)
    d = h - mu
    var = jnp.mean(d * d, axis=-1, keepdims=True)
    return d * lax.rsqrt(var + LN_EPS) * g + b


def _outproj_kernel(yr_ref, yl_ref, ys_ref, x_ref, w_ref, g_ref, b_ref, rw_ref, rb_ref,
                    x1_ref, g4_ref, grp_ref, cnt_ref):
    r0, r1 = RWKV_WIDTH, RWKV_WIDTH + LRU_WIDTH
    mix = (_dot(yr_ref[...].astype(BF16), w_ref[0:r0, :])
           + _dot(yl_ref[...].astype(BF16), w_ref[r0:r1, :])
           + _dot(ys_ref[...].astype(BF16), w_ref[r1:, :]))
    x1 = _layer_norm(ALPHA * x_ref[...] + mix, g_ref[...], b_ref[...])
    x1_ref[...] = x1

    logits = _dot3(rw_ref[...], x1, dot=_dot_nt) + rb_ref[...]
    mx = jnp.max(logits, axis=0, keepdims=True)
    ex = jnp.exp(logits - mx)
    scores = ex / jnp.sum(ex, axis=0, keepdims=True)
    s = [scores[e:e + 1, :] for e in range(N_EXPERTS)]
    top2 = []
    gsum = []
    for grp in range(N_GROUPS):
        mem = list(range(grp * EXPERTS_PER_GROUP, (grp + 1) * EXPERTS_PER_GROUP))
        tot = None
        for e in mem:
            rank = None
            for o in mem:
                if o == e:
                    continue
                ahead = (s[o] > s[e]) | ((s[o] == s[e]) if o < e else False)
                ahead = jnp.where(ahead, 1.0, 0.0)
                rank = ahead if rank is None else rank + ahead
            sel = rank < 1.5
            top2.append(sel)
            part = jnp.where(sel, s[e], 0.0)
            tot = part if tot is None else tot + part
        gsum.append(tot)
    best = gsum[0]
    best_g = jnp.zeros_like(best)
    for grp in range(1, N_GROUPS):
        upd = gsum[grp] > best
        best = jnp.where(upd, gsum[grp], best)
        best_g = jnp.where(upd, float(grp), best_g)
    rows = []
    for k in range(EXPERTS_PER_GROUP):
        acc = None
        for grp in range(N_GROUPS):
            e = grp * EXPERTS_PER_GROUP + k
            part = jnp.where((best_g == float(grp)) & top2[e], s[e] / gsum[grp], 0.0)
            acc = part if acc is None else acc + part
        rows.append(acc)
    g4_ref[...] = jnp.concatenate(rows, axis=0)
    grp_ref[...] = best_g.astype(jnp.int32)
    lane = lax.broadcasted_iota(jnp.int32, (1, LANES), 1)
    cnt = jnp.zeros((1, LANES), F32)
    for grp in range(N_GROUPS):
        n_grp = jnp.sum(jnp.where(best_g == float(grp), 1.0, 0.0), axis=1, keepdims=True)
        cnt = cnt + jnp.where(lane == grp, n_grp, 0.0)
    cnt_ref[0] = cnt.astype(jnp.int32)


def _outproj(y_r, y_l, y_s, x2, w_bf16, ln_g, ln_b, router_wt, router_b, tm=ROUTE_TILE):
    t, d = x2.shape
    tm = min(tm, t)
    row = lambda width: pl.BlockSpec((tm, width), lambda i: (i, 0))
    full = lambda shape: pl.BlockSpec(shape, lambda i: (0,) * len(shape))
    return pl.pallas_call(
        _outproj_kernel,
        grid=(t // tm,),
        in_specs=[row(RWKV_WIDTH), row(LRU_WIDTH), row(SBA_WIDTH), row(d), full((d, d)),
                  full((1, d)), full((1, d)), full((N_EXPERTS, d)), full((N_EXPERTS, 1))],
        out_specs=[row(d), pl.BlockSpec((EXPERTS_PER_GROUP, tm), lambda i: (0, i)),
                   pl.BlockSpec((1, tm), lambda i: (0, i)),
                   pl.BlockSpec((1, 1, LANES), lambda i: (i, 0, 0))],
        out_shape=[jax.ShapeDtypeStruct((t, d), F32),
                   jax.ShapeDtypeStruct((EXPERTS_PER_GROUP, t), F32),
                   jax.ShapeDtypeStruct((1, t), jnp.int32),
                   jax.ShapeDtypeStruct((t // tm, 1, LANES), jnp.int32)],
        compiler_params=_params(("parallel",)),
        name="outproj_ln_router",
    )(y_r, y_l, y_s, x2, w_bf16, ln_g.reshape(1, d), ln_b.reshape(1, d), router_wt,
      router_b.reshape(N_EXPERTS, 1))


def _moe_kernel(cnt_ref, x_ref, grp_ref, g4_ref, wg_ref, wu_ref, wd_ref, g_ref, b_ref, o_ref,
                xs_ref, gs_ref, ys_ref, dest_ref, *, tm, rb, nsub):
    i = pl.program_id(0)
    e = pl.program_id(1)
    cap = xs_ref.shape[0]
    chunks = tm // LANES
    n = [sum(cnt_ref[i * nsub + s, grp] for s in range(nsub)) for grp in range(N_GROUPS)]
    start = [0]
    for grp in range(1, N_GROUPS):
        start.append(start[-1] + n[grp - 1])

    @pl.when(e == 0)
    def _():
        grp = grp_ref[...]
        onehot = jnp.concatenate([jnp.where(grp == k, 1.0, 0.0) for k in range(N_GROUPS)], axis=0)
        li = lax.broadcasted_iota(jnp.int32, (LANES, LANES), 0)
        lj = lax.broadcasted_iota(jnp.int32, (LANES, LANES), 1)
        before = jnp.where(li < lj, 1.0, 0.0).astype(BF16)
        within = _dot(onehot.astype(BF16), before)
        tot = jnp.broadcast_to(jnp.sum(onehot, axis=1, keepdims=True), onehot.shape)
        ri = lax.broadcasted_iota(jnp.int32, (N_GROUPS * chunks, N_GROUPS * chunks), 0)
        ci = lax.broadcasted_iota(jnp.int32, (N_GROUPS * chunks, N_GROUPS * chunks), 1)
        earlier = jnp.where(((ri // chunks) == (ci // chunks)) & (ci < ri), 1.0, 0.0).astype(BF16)
        rank = within + _dot(earlier, tot.astype(BF16))
        dest = jnp.zeros((chunks, LANES), F32)
        for k in range(N_GROUPS):
            dest = jnp.where(grp == k, rank[k * chunks:(k + 1) * chunks] + jnp.asarray(start[k]).astype(F32), dest)
        dest_ref[...] = dest
        dest_i = dest.astype(jnp.int32)
        r_iota = lax.broadcasted_iota(jnp.int32, (tm, LANES), 0)
        place = jnp.concatenate(
            [jnp.where(r_iota == dest_i[c:c + 1, :], 1.0, 0.0).astype(BF16) for c in range(chunks)],
            axis=1)
        xs_ref[0:tm, :] = _dot(place, x_ref[...].astype(BF16)).astype(BF16)
        xs_ref[tm:cap, :] = jnp.zeros((cap - tm, xs_ref.shape[1]), BF16)
        pieces = [p.astype(F32) for p in _split3(g4_ref[...])]
        gates_t = jnp.transpose(jnp.concatenate(
            pieces + [jnp.zeros((LANES - 3 * EXPERTS_PER_GROUP, tm), F32)], axis=0))
        gs_ref[0:tm, :] = _dot(place, gates_t.astype(BF16))
        gs_ref[tm:cap, :] = jnp.zeros((cap - tm, LANES), F32)
        ys_ref[...] = jnp.zeros_like(ys_ref)

    e0 = e * MOE_EXPERTS_PER_STEP
    grp_e = e0 // EXPERTS_PER_GROUP
    n_e = n[0]
    s_e = start[0]
    for grp in range(1, N_GROUPS):
        n_e = jnp.where(grp_e == grp, n[grp], n_e)
        s_e = jnp.where(grp_e == grp, start[grp], s_e)
    lane = lax.broadcasted_iota(jnp.int32, (1, LANES), 1)
    s_al = s_e // MOE_ALIGN * MOE_ALIGN

    def block(b, carry):
        r0 = pl.multiple_of(s_al + b * rb, MOE_ALIGN)
        xb = xs_ref[pl.ds(r0, rb), :]
        gates = gs_ref[pl.ds(r0, rb), :]
        rid = r0 + lax.broadcasted_iota(jnp.int32, (rb, 1), 0)
        inside = (rid >= s_e) & (rid < s_e + n_e)
        acc = None
        for j in range(MOE_EXPERTS_PER_STEP):
            k_e = e0 % EXPERTS_PER_GROUP + j
            mine = ((lane % EXPERTS_PER_GROUP) == k_e) & (lane < 3 * EXPERTS_PER_GROUP)
            gate = jnp.where(inside, jnp.sum(jnp.where(mine, gates, 0.0), axis=1, keepdims=True), 0.0)
            hg = _dot(xb, wg_ref[j])
            hu = _dot(xb, wu_ref[j])
            h = hg * _sigmoid(hg) * hu * gate
            y = _dot(h.astype(BF16), wd_ref[j])
            acc = y if acc is None else acc + y
        ys_ref[pl.ds(r0, rb), :] += acc
        return carry

    lax.fori_loop(0, jnp.where(n_e > 0, (s_e + n_e - s_al + rb - 1) // rb, 0), block, 0)

    @pl.when(e == pl.num_programs(1) - 1)
    def _():
        dest = dest_ref[...]
        dest_t = jnp.transpose(jnp.concatenate([dest, jnp.zeros((LANES - chunks, LANES), F32)], axis=0))
        dest_t = dest_t.astype(jnp.int32)
        c_iota = lax.broadcasted_iota(jnp.int32, (LANES, tm), 1)
        unplace = jnp.concatenate(
            [jnp.where(c_iota == dest_t[:, c:c + 1], 1.0, 0.0).astype(BF16) for c in range(chunks)],
            axis=0)
        moe = _dot(unplace, ys_ref[0:tm, :].astype(BF16))
        o_ref[...] = _layer_norm(ALPHA * x_ref[...] + moe, g_ref[...], b_ref[...])


def _moe(x1, g4, grp, cnt, wg, wu, wd, ln_g, ln_b, tm=MOE_TILE, rb=MOE_ROWS):
    t, d = x1.shape
    de = wg.shape[2]
    tm = min(tm, t)
    nsub = cnt.shape[0] * tm // t
    cap = tm + rb
    eps = MOE_EXPERTS_PER_STEP
    grid_spec = pltpu.PrefetchScalarGridSpec(
        num_scalar_prefetch=1,
        grid=(t // tm, N_EXPERTS // eps),
        in_specs=[pl.BlockSpec((tm, d), lambda i, e, c: (i, 0)),
                  pl.BlockSpec((tm // LANES, LANES), lambda i, e, c: (i, 0)),
                  pl.BlockSpec((EXPERTS_PER_GROUP, tm), lambda i, e, c: (0, i)),
                  pl.BlockSpec((eps, d, de), lambda i, e, c: (e, 0, 0)),
                  pl.BlockSpec((eps, d, de), lambda i, e, c: (e, 0, 0)),
                  pl.BlockSpec((eps, de, d), lambda i, e, c: (e, 0, 0)),
                  pl.BlockSpec((1, d), lambda i, e, c: (0, 0)),
                  pl.BlockSpec((1, d), lambda i, e, c: (0, 0))],
        out_specs=pl.BlockSpec((tm, d), lambda i, e, c: (i, 0)),
        scratch_shapes=[pltpu.VMEM((cap, d), BF16), pltpu.VMEM((cap, LANES), F32),
                        pltpu.VMEM((cap, d), F32), pltpu.VMEM((tm // LANES, LANES), F32)])
    return pl.pallas_call(
        functools.partial(_moe_kernel, tm=tm, rb=rb, nsub=nsub),
        grid_spec=grid_spec,
        out_shape=jax.ShapeDtypeStruct((t, d), F32),
        compiler_params=_params(("parallel", "arbitrary")),
        name="moe_ln",
    )(cnt.reshape(cnt.shape[0], LANES)[:, :N_GROUPS], x1, grp.reshape(t // LANES, LANES), g4,
      wg, wu, wd, ln_g.reshape(1, d), ln_b.reshape(1, d))


def kernel(x, w_in, shift_mu, rwkv_w0, rwkv_w_up, rwkv_a0, rwkv_a_up, rwkv_g_up, rwkv_k_k, rwkv_k_a, rwkv_r_k, rwkv_ln_g, rwkv_ln_b, lru_conv_w, lru_conv_b, lru_wa, lru_ba, lru_wx, lru_bx, lru_lambda, lru_norm_g, sba_norm_g, w_out, ln1_g, ln1_b, ln2_g, ln2_b, router_w, router_b, exp_w_gate, exp_w_up, exp_w_down):
    bsz, seq, d = x.shape
    t = bsz * seq
    x2 = x.reshape(t, d)
    router_wt = router_w.T
    for l in range(w_in.shape[0]):
        p_r, y_l, p_q, p_kt, p_v = _inproj(x2, seq, w_in[l], lru_conv_w[l], lru_conv_b[l], lru_wa[l], lru_ba[l],
                                            lru_wx[l], lru_bx[l], lru_lambda[l], lru_norm_g[l])
        y_r = _rwkv(p_r.reshape(bsz, seq, RWKV_COLS), shift_mu[l], rwkv_w0[l], rwkv_w_up[l],
                    rwkv_a0[l], rwkv_a_up[l], rwkv_g_up[l], rwkv_k_k[l], rwkv_k_a[l],
                    rwkv_r_k[l], rwkv_ln_g[l], rwkv_ln_b[l])
        y_s = _sba(p_q.reshape(bsz, seq, SBA_WIDTH), p_kt, p_v.reshape(bsz, seq, SBA_WIDTH), sba_norm_g[l])
        x1, g4, grp, cnt = _outproj(y_r.reshape(t, RWKV_WIDTH), y_l,
                                    y_s.reshape(t, SBA_WIDTH), x2, w_out[l].astype(BF16),
                                    ln1_g[l], ln1_b[l], router_wt, router_b)
        x2 = _moe(x1, g4, grp, cnt, exp_w_gate[l].astype(BF16), exp_w_up[l].astype(BF16),
                  exp_w_down[l].astype(BF16), ln2_g[l], ln2_b[l])
    return x2.reshape(bsz, seq, d)
```

```python
import functools

import jax
import jax.numpy as jnp
from jax import lax
from jax.experimental import pallas as pl
from jax.experimental.pallas import tpu as pltpu

F32 = jnp.float32
BF16 = jnp.bfloat16

HEAD_DIM = 64
RWKV_WIDTH = 384
LRU_WIDTH = 256
SBA_WIDTH = 384
DECAY_RANK = 64
ICLR_RANK = 64
GATE_RANK = 128
RWKV_COLS = 3 * RWKV_WIDTH + DECAY_RANK + ICLR_RANK + GATE_RANK
LRU_COLS = 2 * LRU_WIDTH
SBA_COLS = 3 * SBA_WIDTH
RWKV_GN_EPS = 64e-5
CONV_WIDTH = 4
LRU_C = 8.0
N_EXPERTS = 16
N_GROUPS = 4
EXPERTS_PER_GROUP = 4
DEPTH = 4
ALPHA = (2.0 * DEPTH) ** 0.25
LN_EPS = 1e-5
RMS_EPS = 1e-6
LOG2E = 1.4426950408889634
SIGN_BIT = -2147483648
SBA_DEAD_LOG2 = -160.0

LANES = 128
MXU_WIDTH = 256
RWKV_CHUNK = 64
SBA_KEY_TILE = 256
ROUTE_TILE = 512
OUTPROJ_PARTS = 4
PROJ_CHUNK = 256
MOE_TILE = 1024
MOE_ROWS = 288
MOE_ALIGN = 16
MOE_EXPERTS_PER_STEP = 2
VMEM_LIMIT = 48 * 1024 * 1024


def _dot(a, b):
    return jnp.dot(a, b, preferred_element_type=F32)


def _dot_nt(a, b):
    return lax.dot_general(a, b, (((1,), (1,)), ((), ())), preferred_element_type=F32)


def _dot_tn(a, b):
    return lax.dot_general(a, b, (((0,), (0,)), ((), ())), preferred_element_type=F32)


def _split3(x):
    hi = x.astype(BF16)
    r1 = x - hi.astype(F32)
    mid = r1.astype(BF16)
    lo = (r1 - mid.astype(F32)).astype(BF16)
    return hi, mid, lo


def _dot_sel_rhs(x, sel):
    hi, mid, lo = _split3(x)
    return _dot(hi, sel) + _dot(mid, sel) + _dot(lo, sel)


def _dot_sel_lhs(sel, x):
    hi, mid, lo = _split3(x)
    return _dot(sel, hi) + _dot(sel, mid) + _dot(sel, lo)


def _dot3(a, b, dot=_dot):
    ah, am, al = _split3(a)
    bh, bm, bl = _split3(b)
    return (dot(ah, bh) + (dot(ah, bm) + dot(am, bh))
            + (dot(ah, bl) + dot(am, bm) + dot(al, bh)))


def _softplus(x):
    return jnp.maximum(x, 0.0) + jnp.log1p(jnp.exp(-jnp.abs(x)))


def _sigmoid(x):
    return 1.0 / (1.0 + jnp.exp(-x))


def _params(sem):
    return pltpu.CompilerParams(dimension_semantics=sem, vmem_limit_bytes=VMEM_LIMIT)


def _rglru_stages(p, first, cw_ref, cb_ref, wax_ref, bax_ref, lam_ref, ng_ref, ones_ref, o_ref, xs_ref, h_ref):
    tb = p.shape[0]
    w = LRU_WIDTH
    hist = 8
    xs_ref[0:hist, :] = jnp.where(first, 0.0, xs_ref[tb:tb + hist, :])
    gate_in = p[:, 0:w]
    xin = p[:, w:2 * w]
    xs_ref[hist:hist + tb, :] = xin
    xc = cb_ref[...] + cw_ref[CONV_WIDTH - 1:CONV_WIDTH, :] * xin
    for j in range(1, CONV_WIDTH):
        xc = xc + cw_ref[CONV_WIDTH - 1 - j:CONV_WIDTH - j, :] * xs_ref[hist - j:hist - j + tb, :]
    gates = _dot(xc.astype(BF16), wax_ref[...]) + bax_ref[...]
    yield
    gate_a = _sigmoid(gates[:, 0:w])
    gate_x = _sigmoid(gates[:, w:2 * w])
    log_a = -LRU_C * gate_a * _softplus(-lam_ref[...])
    a = jnp.exp(log_a)
    bb = jnp.sqrt(1.0 - jnp.exp(2.0 * log_a)) * (gate_x * xc)
    yield
    row = lax.broadcasted_iota(jnp.int32, (tb, 1), 0)
    d = 1
    while d < tb:
        keep = row >= d
        a_s = jnp.where(keep, pltpu.roll(a, d, axis=0), 1.0)
        b_s = jnp.where(keep, pltpu.roll(bb, d, axis=0), 0.0)
        bb = a * b_s + bb
        a = a * a_s
        d *= 2
        yield
    h = bb + a * jnp.where(first, 0.0, h_ref[0:1, :])
    h_ref[0:1, :] = h[tb - 1:tb, :]
    gelu = 0.5 * gate_in * (1.0 + jnp.tanh(0.7978845608028654 * (gate_in + 0.044715 * (gate_in * gate_in * gate_in))))
    y = gelu * h
    ms = _dot((y * y).astype(BF16), ones_ref[...]) * (1.0 / HEAD_DIM)
    o_ref[...] = y * lax.rsqrt(ms + RMS_EPS) * ng_ref[...]


def _inproj_kernel(x_ref, w_ref, wkt_ref, cw_ref, cb_ref, wax_ref, bax_ref, lam_ref, ng_ref, ones_ref,
                   o_r, o_yl, o_q, o_kt, o_v, xs_ref, h_ref, *, tiles_per_seq):
    xb = x_ref[...].astype(BF16)
    first = pl.program_id(0) % tiles_per_seq == 0
    p_l = _dot(xb, w_ref[:, RWKV_COLS:RWKV_COLS + LRU_COLS])
    lru = _rglru_stages(p_l, first, cw_ref, cb_ref, wax_ref, bax_ref, lam_ref, ng_ref, ones_ref, o_yl,
                        xs_ref, h_ref)

    def project(o_ref, col, c0, c1):
        o_ref[:, c0:c1] = _dot(xb, w_ref[:, col + c0:col + c1]).astype(o_ref.dtype)

    tk = o_kt.shape[2]

    def project_keys(r0, r1):
        kt = _dot_nt(wkt_ref[r0:r1, :], xb)
        for i in range(o_kt.shape[0]):
            o_kt[i, r0:r1, :] = kt[:, i * tk:(i + 1) * tk].astype(o_kt.dtype)

    work = []
    col = 0
    for o_ref, skip in ((o_r, LRU_COLS), (o_q, SBA_WIDTH), (o_v, 0)):
        width = o_ref.shape[1]
        for c0 in range(0, width, PROJ_CHUNK):
            work.append(functools.partial(project, o_ref, col, c0, min(c0 + PROJ_CHUNK, width)))
        col += width + skip
    for r0 in range(0, SBA_WIDTH, LANES):
        work.append(functools.partial(project_keys, r0, r0 + LANES))
    for step in work:
        step()
        next(lru, None)
    for _ in lru:
        pass


def _block_diag(wb):
    n, d, _ = wb.shape
    eye = jnp.eye(n, dtype=wb.dtype)
    return jnp.einsum('gij,gh->gihj', wb, eye).reshape(n * d, n * d)


def _inproj(x2, seq, w, conv_w, conv_b, wa, ba, wx, bx, lam, norm_g, tm=512, tk=SBA_KEY_TILE):
    t, d = x2.shape
    n = w.shape[1]
    tm = min(tm, seq)
    lw = LRU_WIDTH
    q0 = RWKV_COLS + LRU_COLS
    k0 = q0 + SBA_WIDTH
    col = jnp.arange(n)
    w_bf16 = (w * jnp.where((col >= q0) & (col < k0), LOG2E * HEAD_DIM ** -0.5, 1.0)).astype(BF16)
    wkt = w_bf16[:, k0:k0 + SBA_WIDTH].T
    wax = jnp.concatenate([_block_diag(wa), _block_diag(wx)], axis=1).astype(BF16)
    bax = jnp.concatenate([ba, bx]).reshape(1, 2 * lw)
    head = jnp.arange(lw) // HEAD_DIM
    ones = (head[:, None] == head[None, :]).astype(BF16)
    full = lambda shape: pl.BlockSpec(shape, lambda i: (0,) * len(shape))
    return pl.pallas_call(
        functools.partial(_inproj_kernel, tiles_per_seq=seq // tm),
        grid=(t // tm,),
        in_specs=[pl.BlockSpec((tm, d), lambda i: (i, 0)), full((d, n)), full((SBA_WIDTH, d)),
                  full((CONV_WIDTH, lw)), full((1, lw)), full((lw, 2 * lw)), full((1, 2 * lw)),
                  full((1, lw)), full((1, lw)), full((lw, lw))],
        out_specs=[pl.BlockSpec((tm, RWKV_COLS), lambda i: (i, 0)),
                   pl.BlockSpec((tm, lw), lambda i: (i, 0)),
                   pl.BlockSpec((tm, SBA_WIDTH), lambda i: (i, 0)),
                   pl.BlockSpec((tm // tk, SBA_WIDTH, tk), lambda i: (i, 0, 0)),
                   pl.BlockSpec((tm, SBA_WIDTH), lambda i: (i, 0))],
        out_shape=[jax.ShapeDtypeStruct((t, RWKV_COLS), F32),
                   jax.ShapeDtypeStruct((t, lw), F32),
                   jax.ShapeDtypeStruct((t, SBA_WIDTH), BF16),
                   jax.ShapeDtypeStruct((t // tk, SBA_WIDTH, tk), BF16),
                   jax.ShapeDtypeStruct((t, SBA_WIDTH), BF16)],
        scratch_shapes=[pltpu.VMEM((tm + 8, lw), F32), pltpu.VMEM((8, lw), F32)],
        compiler_params=_params(("arbitrary",)),
        name="inproj_rglru",
    )(x2, w_bf16, wkt, conv_w, conv_b.reshape(1, lw), wax, bax, lam.reshape(1, lw), norm_g.reshape(1, lw), ones)


def _rwkv_kernel(p_ref, mu_ref, w0_ref, a0_ref, kk_ref, ka_ref, rk_ref, lng_ref, lnb_ref,
                 wup_ref, aup_ref, gup_ref, ones_ref, o_ref, carry_ref, ht_ref, *, nb):
    c = RWKV_CHUNK
    w = RWKV_WIDTH
    rows = nb * c
    t = pl.program_id(1)

    @pl.when(t == 0)
    def _():
        carry_ref[...] = jnp.zeros_like(carry_ref)
        ht_ref[...] = jnp.zeros_like(ht_ref)

    p = p_ref[...].reshape(rows, RWKV_COLS)
    row = lax.broadcasted_iota(jnp.int32, (rows, 1), 0)
    prev = pltpu.roll(p, 1, axis=0)
    for n in range(nb):
        prev = jnp.where(row == n * c, carry_ref[n:n + 1, :], prev)
    for n in range(nb):
        carry_ref[n:n + 1, :] = p_ref[n, c - 1:c, :]
    ps = p + mu_ref[...] * (prev - p)

    r = ps[:, 0:w]
    k = ps[:, w:2 * w]
    v = ps[:, 2 * w:3 * w]
    lora_in = ps[:, 3 * w:3 * w + LANES]
    dg = ps[:, 3 * w + LANES:3 * w + 2 * LANES]

    w_log = -_softplus(-(w0_ref[...] + _dot(jnp.tanh(lora_in).astype(BF16), wup_ref[...]))) - 0.5
    logw = -jnp.exp(w_log)
    a = _sigmoid(a0_ref[...] + _dot(lora_in.astype(BF16), aup_ref[...]))
    g = _dot(_sigmoid(dg).astype(BF16), gup_ref[...])

    ones = ones_ref[...]

    def seg_sum(x):
        return jnp.concatenate([_dot(x[:, i:i + LANES].astype(BF16), ones) for i in range(0, w, LANES)], axis=1)

    kkr = k * kk_ref[...]
    kk = kkr / jnp.maximum(jnp.sqrt(seg_sum(kkr * kkr)), 1e-12)
    k2 = k * (1.0 + (a - 1.0) * ka_ref[...])
    b = kk * a

    grows = min(rows, MXU_WIDTH)
    ri = lax.broadcasted_iota(jnp.int32, (grows, grows), 0)
    ci = lax.broadcasted_iota(jnp.int32, (grows, grows), 1)
    tri = jnp.where(((ri // c) == (ci // c)) & (ci <= ri), 1.0, 0.0).astype(BF16)
    lw_hi = logw.astype(BF16)
    lw_lo = (logw - lw_hi.astype(F32)).astype(BF16)
    cum = jnp.concatenate([_dot(tri, lw_hi[i:i + grows]) + _dot(tri, lw_lo[i:i + grows])
                           for i in range(0, rows, grows)], axis=0)

    lane = lax.broadcasted_iota(jnp.int32, (1, LANES), 1)
    first = lane < HEAD_DIM

    def stack(x):
        return jnp.concatenate([jnp.where(first, x, 0.0), jnp.where(first, 0.0, x)], axis=0)

    ri2 = lax.broadcasted_iota(jnp.int32, (2 * c, 2 * c), 0)
    ci2 = lax.broadcasted_iota(jnp.int32, (2 * c, 2 * c), 1)
    same = (ri2 // c) == (ci2 // c)
    strict = same & (ci2 < ri2)
    incl = same & (ci2 <= ri2)
    eye = jnp.where(ri2 == ci2, 1.0, 0.0)

    def dotb(x, y):
        return _dot(x.astype(BF16), y.astype(BF16))

    cum_tot = jnp.concatenate(
        [jnp.broadcast_to(cum[n * c + c - 1:n * c + c, :], (c, w)) for n in range(nb)], axis=0)
    rt = r * jnp.exp(cum)
    at = -kk * jnp.exp(cum - logw)
    e_neg = jnp.exp(-cum)
    bt = b * e_neg
    kt = k2 * e_neg
    e_rem = jnp.exp(cum_tot - cum)
    bh = b * e_rem
    kh = k2 * e_rem
    e_tot = jnp.exp(cum_tot)

    chains = [(n, j) for n in range(nb) for j in range(w // LANES)]

    def cut(x, n, j):
        return x[n * c:(n + 1) * c, j * LANES:(j + 1) * LANES]

    ar = [jnp.concatenate([stack(cut(at, n, j)), stack(cut(rt, n, j))], axis=0).astype(BF16)
          for n, j in chains]
    bk = [jnp.concatenate([stack(cut(bt, n, j)), stack(cut(kt, n, j))], axis=0).astype(BF16)
          for n, j in chains]
    gram = [_dot_nt(x, y) for x, y in zip(ar, bk)]
    ht = [ht_ref[n, j] for n, j in chains]
    xh = [_dot_nt(x, h.astype(BF16)) for x, h in zip(ar, ht)]
    v_b = [stack(cut(v, n, j)).astype(BF16) for n, j in chains]
    a_ab = [jnp.where(strict, gm[0:2 * c, 0:2 * c], 0.0) for gm in gram]
    a_ak = [jnp.where(strict, gm[0:2 * c, 2 * c:4 * c], 0.0).astype(BF16) for gm in gram]
    a_r = [jnp.concatenate([jnp.where(incl, gm[2 * c:4 * c, 0:2 * c], 0.0),
                            jnp.where(incl, gm[2 * c:4 * c, 2 * c:4 * c], 0.0)], axis=1).astype(BF16)
           for gm in gram]
    rhs_u = [x[0:2 * c] + _dot(ak, vb) for x, ak, vb in zip(xh, a_ak, v_b)]

    pw = [x.astype(BF16) for x in a_ab]
    inv = [eye + x for x in a_ab]
    for _ in range(5):
        pw_f = [_dot(x, x) for x in pw]
        pw = [x.astype(BF16) for x in pw_f]
        inv = [x + _dot(x.astype(BF16), y) for x, y in zip(inv, pw)]

    u_d = [dotb(x, y) for x, y in zip(inv, rhs_u)]
    uv = [jnp.concatenate([u.astype(BF16), vb], axis=0) for u, vb in zip(u_d, v_b)]
    o_d = [x[2 * c:4 * c] + _dot(a, y) for x, a, y in zip(xh, a_r, uv)]
    bkh = [jnp.concatenate([stack(cut(bh, n, j)), stack(cut(kh, n, j))], axis=0).astype(BF16)
           for n, j in chains]
    upd = [_dot_tn(y, z) for y, z in zip(uv, bkh)]
    for (n, j), h, u in zip(chains, ht, upd):
        ht_ref[n, j] = h * e_tot[n * c:n * c + 1, j * LANES:(j + 1) * LANES] + u
    o_pair = [x[0:c] + x[c:2 * c] for x in o_d]
    npairs = w // LANES
    o = jnp.concatenate([jnp.concatenate(o_pair[n * npairs:(n + 1) * npairs], axis=1)
                         for n in range(nb)], axis=0)
    inv_n = 1.0 / HEAD_DIM
    d = o - seg_sum(o) * inv_n
    var = seg_sum(d * d) * inv_n
    on = d * lax.rsqrt(var + RWKV_GN_EPS) * lng_ref[...] + lnb_ref[...]
    bonus = seg_sum(r * k2 * rk_ref[...]) * v
    o_ref[...] = ((on + bonus) * g).reshape(nb, c, w)


def _rwkv(p_r, mu, w0, wup, a0, aup, gup, k_k, k_a, r_k, ln_g, ln_b, nb=8):
    bsz, seq, _ = p_r.shape
    nb = min(nb, bsz)
    c = RWKV_CHUNK
    w = RWKV_WIDTH
    zeros = jnp.zeros((DECAY_RANK, w), F32)
    wup_pad = jnp.concatenate([wup, zeros], axis=0).astype(BF16)
    aup_pad = jnp.concatenate([zeros, aup], axis=0).astype(BF16)
    head = jnp.arange(LANES) // HEAD_DIM
    ones = (head[:, None] == head[None, :]).astype(BF16)
    vec = lambda x: x.reshape(1, -1)
    full = lambda shape: pl.BlockSpec(shape, lambda b, t: (0,) * len(shape))
    return pl.pallas_call(
        functools.partial(_rwkv_kernel, nb=nb),
        grid=(bsz // nb, seq // c),
        in_specs=[pl.BlockSpec((nb, c, RWKV_COLS), lambda b, t: (b, t, 0)),
                  full((1, RWKV_COLS))] + [full((1, w))] * 7
                 + [full((LANES, w)), full((LANES, w)), full((GATE_RANK, w)), full((LANES, LANES))],
        out_specs=pl.BlockSpec((nb, c, w), lambda b, t: (b, t, 0)),
        out_shape=jax.ShapeDtypeStruct((bsz, seq, w), F32),
        scratch_shapes=[pltpu.VMEM((8, RWKV_COLS), F32),
                        pltpu.VMEM((nb, w // LANES, LANES, LANES), F32)],
        compiler_params=_params(("parallel", "arbitrary")),
        name="rwkv7",
    )(p_r, vec(mu), vec(w0), vec(a0), vec(k_k), vec(k_a), vec(r_k), vec(ln_g), vec(ln_b),
      wup_pad, aup_pad, gup.astype(BF16), ones)


def _sba_kernel(q_ref, kt_ref, v_ref, g_ref, cs_ref, ones_ref, o_ref, acc_ref, car_ref, *, tq, tk):
    qi = pl.program_id(2)
    nsub = tq // tk
    lane = lax.broadcasted_iota(jnp.int32, (1, LANES), 1)
    first = lane < HEAD_DIM
    q = q_ref[0]
    zero = jnp.zeros_like(q)
    qh = (jnp.where(first, q, zero), jnp.where(first, zero, q))
    cs = cs_ref[...]
    acc_ref[...] = jnp.zeros_like(acc_ref)
    car_ref[...] = jnp.zeros_like(car_ref)

    def tile(j, r0, masked):
        kt = kt_ref[j]
        vt = v_ref[0, pl.ds(pl.multiple_of(j * tk, tk), tk), :]
        if masked:
            ti = lax.broadcasted_iota(jnp.int32, (tq - r0, tk), 0)
            si = lax.broadcasted_iota(jnp.int32, (tq - r0, tk), 1)
            mask = si < ti
        for h in range(2):
            z = _dot(qh[h][r0:], kt)
            neg_abs = lax.bitcast_convert_type(lax.bitcast_convert_type(z, jnp.int32) | SIGN_BIT, F32)
            nk = jnp.maximum(z, 0.0) + jnp.log2(1.0 + jnp.exp2(neg_abs))
            if masked:
                nk = jnp.where(mask, nk, 0.0)
            after = _dot(nk.astype(BF16), cs)
            car = car_ref[h, r0:, :]
            attn = jnp.exp2((z - nk) + after + car)
            if masked:
                attn = jnp.where(mask, attn, 0.0)
            acc_ref[h, r0:, :] += _dot(attn.astype(BF16), vt)
            car_ref[h, r0:, :] = car + (after[:, 0:1] - nk[:, 0:1])

    for d in range(nsub - 1, -1, -1):
        tile(qi * nsub + d, d * tk, True)

    def alive():
        return (jnp.max(car_ref[...]) >= SBA_DEAD_LOG2).astype(jnp.int32)

    def cond(state):
        i, live = state
        return (i < qi * nsub) & (live > 0)

    def body(state):
        i, _ = state
        tile(qi * nsub - 1 - i, 0, False)
        return i + 1, alive()

    lax.while_loop(cond, body, (jnp.int32(0), alive()))
    o = jnp.where(first, acc_ref[0], acc_ref[1])
    ms = _dot((o * o).astype(BF16), ones_ref[...]) * (1.0 / HEAD_DIM)
    o_ref[0] = o * lax.rsqrt(ms + RMS_EPS) * g_ref[...]


def _sba(q, kt, v, norm_g, tq=512):
    bsz, seq, _ = q.shape
    tk = kt.shape[2]
    tq = min(tq, seq)
    pairs = SBA_WIDTH // LANES
    ji = jnp.arange(tk)
    cs = -(ji[:, None] > ji[None, :]).astype(BF16)
    head = jnp.arange(LANES) // HEAD_DIM
    ones = (head[:, None] == head[None, :]).astype(BF16)
    return pl.pallas_call(
        functools.partial(_sba_kernel, tq=tq, tk=tk),
        grid=(bsz, pairs, seq // tq),
        in_specs=[pl.BlockSpec((1, tq, LANES), lambda b, h, i: (b, i, h)),
                  pl.BlockSpec((seq // tk, LANES, tk), lambda b, h, i: (b, h, 0)),
                  pl.BlockSpec((1, seq, LANES), lambda b, h, i: (b, 0, h)),
                  pl.BlockSpec((1, LANES), lambda b, h, i: (0, h)),
                  pl.BlockSpec((tk, tk), lambda b, h, i: (0, 0)),
                  pl.BlockSpec((LANES, LANES), lambda b, h, i: (0, 0))],
        out_specs=pl.BlockSpec((1, tq, LANES), lambda b, h, i: (b, i, h)),
        out_shape=jax.ShapeDtypeStruct((bsz, seq, SBA_WIDTH), F32),
        scratch_shapes=[pltpu.VMEM((2, tq, LANES), F32), pltpu.VMEM((2, tq, 1), F32)],
        compiler_params=_params(("parallel", "parallel", "arbitrary")),
        name="stickbreak",
    )(q, kt, v, norm_g.reshape(1, SBA_WIDTH), cs, ones)


def _layer_norm(h, g, b):
    mu = jnp.mean(h, axis=-1, keepdims=True)
    d = h - mu
    var = jnp.mean(d * d, axis=-1, keepdims=True)
    return d * lax.rsqrt(var + LN_EPS) * g + b


def _route(scores):
    s = [scores[e:e + 1, :] for e in range(N_EXPERTS)]
    top2 = []
    gsum = []
    for grp in range(N_GROUPS):
        mem = list(range(grp * EXPERTS_PER_GROUP, (grp + 1) * EXPERTS_PER_GROUP))
        tot = None
        for e in mem:
            rank = None
            for o in mem:
                if o == e:
                    continue
                ahead = (s[o] > s[e]) | ((s[o] == s[e]) if o < e else False)
                ahead = jnp.where(ahead, 1.0, 0.0)
                rank = ahead if rank is None else rank + ahead
            sel = rank < 1.5
            top2.append(sel)
            part = jnp.where(sel, s[e], 0.0)
            tot = part if tot is None else tot + part
        gsum.append(tot)
    best = gsum[0]
    best_g = jnp.zeros_like(best)
    for grp in range(1, N_GROUPS):
        upd = gsum[grp] > best
        best = jnp.where(upd, gsum[grp], best)
        best_g = jnp.where(upd, float(grp), best_g)
    rows = []
    for k in range(EXPERTS_PER_GROUP):
        acc = None
        for grp in range(N_GROUPS):
            e = grp * EXPERTS_PER_GROUP + k
            part = jnp.where((best_g == float(grp)) & top2[e], s[e] / gsum[grp], 0.0)
            acc = part if acc is None else acc + part
        rows.append(acc)
    return jnp.concatenate(rows, axis=0), best_g


def _outproj_kernel(yr_ref, yl_ref, ys_ref, x_ref, w_ref, g_ref, b_ref, rw_ref, rb_ref,
                    x1_ref, g4_ref, grp_ref, cnt_ref):
    tm = x_ref.shape[0]
    part = tm // OUTPROJ_PARTS
    spans = [(i * part, (i + 1) * part) for i in range(OUTPROJ_PARTS)]
    r0, r1 = RWKV_WIDTH, RWKV_WIDTH + LRU_WIDTH
    x1 = []
    for a, b in spans:
        mix = (_dot(yr_ref[a:b, :].astype(BF16), w_ref[0:r0, :])
               + _dot(yl_ref[a:b, :].astype(BF16), w_ref[r0:r1, :])
               + _dot(ys_ref[a:b, :].astype(BF16), w_ref[r1:, :]))
        x1.append(_layer_norm(ALPHA * x_ref[a:b, :] + mix, g_ref[...], b_ref[...]))
        x1_ref[a:b, :] = x1[-1]
    rw_hi = rw_ref[...].astype(BF16)
    rw_lo = (rw_ref[...] - rw_hi.astype(F32)).astype(BF16)
    logits = []
    for v in x1:
        hi = v.astype(BF16)
        lo = (v - hi.astype(F32)).astype(BF16)
        logits.append(_dot_nt(rw_hi, hi) + (_dot_nt(rw_hi, lo) + _dot_nt(rw_lo, hi)) + rb_ref[...])
    lane = lax.broadcasted_iota(jnp.int32, (1, LANES), 1)
    cnt = jnp.zeros((1, LANES), F32)
    for (a, b), lg in zip(spans, logits):
        ex = jnp.exp(lg - jnp.max(lg, axis=0, keepdims=True))
        gates, best_g = _route(ex / jnp.sum(ex, axis=0, keepdims=True))
        g4_ref[:, a:b] = gates
        grp_ref[:, a:b] = best_g.astype(jnp.int32)
        for grp in range(N_GROUPS):
            n_grp = jnp.sum(jnp.where(best_g == float(grp), 1.0, 0.0), axis=1, keepdims=True)
            cnt = cnt + jnp.where(lane == grp, n_grp, 0.0)
    cnt_ref[0] = cnt.astype(jnp.int32)


def _outproj(y_r, y_l, y_s, x2, w_bf16, ln_g, ln_b, router_wt, router_b, tm=ROUTE_TILE):
    t, d = x2.shape
    tm = min(tm, t)
    row = lambda width: pl.BlockSpec((tm, width), lambda i: (i, 0))
    full = lambda shape: pl.BlockSpec(shape, lambda i: (0,) * len(shape))
    return pl.pallas_call(
        _outproj_kernel,
        grid=(t // tm,),
        in_specs=[row(RWKV_WIDTH), row(LRU_WIDTH), row(SBA_WIDTH), row(d), full((d, d)),
                  full((1, d)), full((1, d)), full((N_EXPERTS, d)), full((N_EXPERTS, 1))],
        out_specs=[row(d), pl.BlockSpec((EXPERTS_PER_GROUP, tm), lambda i: (0, i)),
                   pl.BlockSpec((1, tm), lambda i: (0, i)),
                   pl.BlockSpec((1, 1, LANES), lambda i: (i, 0, 0))],
        out_shape=[jax.ShapeDtypeStruct((t, d), F32),
                   jax.ShapeDtypeStruct((EXPERTS_PER_GROUP, t), F32),
                   jax.ShapeDtypeStruct((1, t), jnp.int32),
                   jax.ShapeDtypeStruct((t // tm, 1, LANES), jnp.int32)],
        compiler_params=_params(("parallel",)),
        name="outproj_ln_router",
    )(y_r, y_l, y_s, x2, w_bf16, ln_g.reshape(1, d), ln_b.reshape(1, d), router_wt,
      router_b.reshape(N_EXPERTS, 1))


def _moe_kernel(cnt_ref, x_ref, grp_ref, g4_ref, wg_ref, wu_ref, wd_ref, g_ref, b_ref, o_ref,
                xs_ref, gs_ref, ys_ref, dest_ref, *, tm, rb, nsub):
    i = pl.program_id(0)
    e = pl.program_id(1)
    cap = xs_ref.shape[0]
    chunks = tm // LANES
    n = [sum(cnt_ref[i * nsub + s, grp] for s in range(nsub)) for grp in range(N_GROUPS)]
    start = [0]
    for grp in range(1, N_GROUPS):
        start.append(start[-1] + n[grp - 1])

    @pl.when(e == 0)
    def _():
        grp = grp_ref[...]
        onehot = jnp.concatenate([jnp.where(grp == k, 1.0, 0.0) for k in range(N_GROUPS)], axis=0)
        li = lax.broadcasted_iota(jnp.int32, (LANES, LANES), 0)
        lj = lax.broadcasted_iota(jnp.int32, (LANES, LANES), 1)
        before = jnp.where(li < lj, 1.0, 0.0).astype(BF16)
        within = _dot(onehot.astype(BF16), before)
        tot = jnp.broadcast_to(jnp.sum(onehot, axis=1, keepdims=True), onehot.shape)
        ri = lax.broadcasted_iota(jnp.int32, (N_GROUPS * chunks, N_GROUPS * chunks), 0)
        ci = lax.broadcasted_iota(jnp.int32, (N_GROUPS * chunks, N_GROUPS * chunks), 1)
        earlier = jnp.where(((ri // chunks) == (ci // chunks)) & (ci < ri), 1.0, 0.0).astype(BF16)
        rank = within + _dot(earlier, tot.astype(BF16))
        dest = jnp.zeros((chunks, LANES), F32)
        for k in range(N_GROUPS):
            dest = jnp.where(grp == k, rank[k * chunks:(k + 1) * chunks] + jnp.asarray(start[k]).astype(F32), dest)
        dest_ref[...] = dest
        dest_i = dest.astype(jnp.int32)
        r_iota = lax.broadcasted_iota(jnp.int32, (tm, LANES), 0)
        place = jnp.concatenate(
            [jnp.where(r_iota == dest_i[c:c + 1, :], 1.0, 0.0).astype(BF16) for c in range(chunks)],
            axis=1)
        xs_ref[0:tm, :] = _dot(place, x_ref[...].astype(BF16)).astype(BF16)
        xs_ref[tm:cap, :] = jnp.zeros((cap - tm, xs_ref.shape[1]), BF16)
        pieces = [p.astype(F32) for p in _split3(g4_ref[...])]
        gates_t = jnp.transpose(jnp.concatenate(
            pieces + [jnp.zeros((LANES - 3 * EXPERTS_PER_GROUP, tm), F32)], axis=0))
        gs_ref[0:tm, :] = _dot(place, gates_t.astype(BF16))
        gs_ref[tm:cap, :] = jnp.zeros((cap - tm, LANES), F32)
        ys_ref[...] = jnp.zeros_like(ys_ref)

    e0 = e * MOE_EXPERTS_PER_STEP
    grp_e = e0 // EXPERTS_PER_GROUP
    n_e = n[0]
    s_e = start[0]
    for grp in range(1, N_GROUPS):
        n_e = jnp.where(grp_e == grp, n[grp], n_e)
        s_e = jnp.where(grp_e == grp, start[grp], s_e)
    lane = lax.broadcasted_iota(jnp.int32, (1, LANES), 1)
    s_al = s_e // MOE_ALIGN * MOE_ALIGN

    def block(b, carry):
        r0 = pl.multiple_of(s_al + b * rb, MOE_ALIGN)
        xb = xs_ref[pl.ds(r0, rb), :]
        gates = gs_ref[pl.ds(r0, rb), :]
        rid = r0 + lax.broadcasted_iota(jnp.int32, (rb, 1), 0)
        inside = (rid >= s_e) & (rid < s_e + n_e)
        acc = None
        for j in range(MOE_EXPERTS_PER_STEP):
            k_e = e0 % EXPERTS_PER_GROUP + j
            mine = ((lane % EXPERTS_PER_GROUP) == k_e) & (lane < 3 * EXPERTS_PER_GROUP)
            gate = jnp.where(inside, jnp.sum(jnp.where(mine, gates, 0.0), axis=1, keepdims=True), 0.0)
            hg = _dot(xb, wg_ref[j])
            hu = _dot(xb, wu_ref[j])
            h = hg * _sigmoid(hg) * hu * gate
            y = _dot(h.astype(BF16), wd_ref[j])
            acc = y if acc is None else acc + y
        ys_ref[pl.ds(r0, rb), :] += acc
        return carry

    lax.fori_loop(0, jnp.where(n_e > 0, (s_e + n_e - s_al + rb - 1) // rb, 0), block, 0)

    @pl.when(e == pl.num_programs(1) - 1)
    def _():
        dest = dest_ref[...]
        dest_t = jnp.transpose(jnp.concatenate([dest, jnp.zeros((LANES - chunks, LANES), F32)], axis=0))
        dest_t = dest_t.astype(jnp.int32)
        c_iota = lax.broadcasted_iota(jnp.int32, (LANES, tm), 1)
        unplace = jnp.concatenate(
            [jnp.where(c_iota == dest_t[:, c:c + 1], 1.0, 0.0).astype(BF16) for c in range(chunks)],
            axis=0)
        moe = _dot(unplace, ys_ref[0:tm, :].astype(BF16))
        o_ref[...] = _layer_norm(ALPHA * x_ref[...] + moe, g_ref[...], b_ref[...])


def _moe(x1, g4, grp, cnt, wg, wu, wd, ln_g, ln_b, tm=MOE_TILE, rb=MOE_ROWS):
    t, d = x1.shape
    de = wg.shape[2]
    tm = min(tm, t)
    nsub = cnt.shape[0] * tm // t
    cap = tm + rb
    eps = MOE_EXPERTS_PER_STEP
    grid_spec = pltpu.PrefetchScalarGridSpec(
        num_scalar_prefetch=1,
        grid=(t // tm, N_EXPERTS // eps),
        in_specs=[pl.BlockSpec((tm, d), lambda i, e, c: (i, 0)),
                  pl.BlockSpec((tm // LANES, LANES), lambda i, e, c: (i, 0)),
                  pl.BlockSpec((EXPERTS_PER_GROUP, tm), lambda i, e, c: (0, i)),
                  pl.BlockSpec((eps, d, de), lambda i, e, c: (e, 0, 0)),
                  pl.BlockSpec((eps, d, de), lambda i, e, c: (e, 0, 0)),
                  pl.BlockSpec((eps, de, d), lambda i, e, c: (e, 0, 0)),
                  pl.BlockSpec((1, d), lambda i, e, c: (0, 0)),
                  pl.BlockSpec((1, d), lambda i, e, c: (0, 0))],
        out_specs=pl.BlockSpec((tm, d), lambda i, e, c: (i, 0)),
        scratch_shapes=[pltpu.VMEM((cap, d), BF16), pltpu.VMEM((cap, LANES), F32),
                        pltpu.VMEM((cap, d), F32), pltpu.VMEM((tm // LANES, LANES), F32)])
    return pl.pallas_call(
        functools.partial(_moe_kernel, tm=tm, rb=rb, nsub=nsub),
        grid_spec=grid_spec,
        out_shape=jax.ShapeDtypeStruct((t, d), F32),
        compiler_params=_params(("parallel", "arbitrary")),
        name="moe_ln",
    )(cnt.reshape(cnt.shape[0], LANES)[:, :N_GROUPS], x1, grp.reshape(t // LANES, LANES), g4,
      wg, wu, wd, ln_g.reshape(1, d), ln_b.reshape(1, d))


def kernel(x, w_in, shift_mu, rwkv_w0, rwkv_w_up, rwkv_a0, rwkv_a_up, rwkv_g_up, rwkv_k_k, rwkv_k_a, rwkv_r_k, rwkv_ln_g, rwkv_ln_b, lru_conv_w, lru_conv_b, lru_wa, lru_ba, lru_wx, lru_bx, lru_lambda, lru_norm_g, sba_norm_g, w_out, ln1_g, ln1_b, ln2_g, ln2_b, router_w, router_b, exp_w_gate, exp_w_up, exp_w_down):
    bsz, seq, d = x.shape
    t = bsz * seq
    x2 = x.reshape(t, d)
    router_wt = router_w.T
    for l in range(w_in.shape[0]):
        p_r, y_l, p_q, p_kt, p_v = _inproj(x2, seq, w_in[l], lru_conv_w[l], lru_conv_b[l], lru_wa[l], lru_ba[l],
                                            lru_wx[l], lru_bx[l], lru_lambda[l], lru_norm_g[l])
        y_r = _rwkv(p_r.reshape(bsz, seq, RWKV_COLS), shift_mu[l], rwkv_w0[l], rwkv_w_up[l],
                    rwkv_a0[l], rwkv_a_up[l], rwkv_g_up[l], rwkv_k_k[l], rwkv_k_a[l],
                    rwkv_r_k[l], rwkv_ln_g[l], rwkv_ln_b[l])
        y_s = _sba(p_q.reshape(bsz, seq, SBA_WIDTH), p_kt, p_v.reshape(bsz, seq, SBA_WIDTH), sba_norm_g[l])
        x1, g4, grp, cnt = _outproj(y_r.reshape(t, RWKV_WIDTH), y_l,
                                    y_s.reshape(t, SBA_WIDTH), x2, w_out[l].astype(BF16),
                                    ln1_g[l], ln1_b[l], router_wt, router_b)
        x2 = _moe(x1, g4, grp, cnt, exp_w_gate[l].astype(BF16), exp_w_up[l].astype(BF16),
                  exp_w_down[l].astype(BF16), ln2_g[l], ln2_b[l])
    return x2.reshape(bsz, seq, d)
```

```python
import functools

import jax
import jax.numpy as jnp
from jax import lax
from jax.experimental import pallas as pl
from jax.experimental.pallas import tpu as pltpu

F32 = jnp.float32
BF16 = jnp.bfloat16

HEAD_DIM = 64
RWKV_WIDTH = 384
LRU_WIDTH = 256
SBA_WIDTH = 384
DECAY_RANK = 64
ICLR_RANK = 64
GATE_RANK = 128
RWKV_COLS = 3 * RWKV_WIDTH + DECAY_RANK + ICLR_RANK + GATE_RANK
LRU_COLS = 2 * LRU_WIDTH
SBA_COLS = 3 * SBA_WIDTH
RWKV_GN_EPS = 64e-5
CONV_WIDTH = 4
LRU_C = 8.0
N_EXPERTS = 16
N_GROUPS = 4
EXPERTS_PER_GROUP = 4
DEPTH = 4
ALPHA = (2.0 * DEPTH) ** 0.25
LN_EPS = 1e-5
RMS_EPS = 1e-6
LOG2E = 1.4426950408889634
SIGN_BIT = -2147483648
SBA_DEAD_LOG2 = -160.0

LANES = 128
MXU_WIDTH = 256
RWKV_CHUNK = 64
SBA_KEY_TILE = 256
ROUTE_TILE = 512
OUTPROJ_PARTS = 4
PROJ_CHUNK = 256
MOE_TILE = 1024
MOE_ROWS = 288
MOE_ALIGN = 16
MOE_EXPERTS_PER_STEP = 4
VMEM_LIMIT = 52 * 1024 * 1024


def _dot(a, b):
    return jnp.dot(a, b, preferred_element_type=F32)


def _dot_nt(a, b):
    return lax.dot_general(a, b, (((1,), (1,)), ((), ())), preferred_element_type=F32)


def _dot_tn(a, b):
    return lax.dot_general(a, b, (((0,), (0,)), ((), ())), preferred_element_type=F32)


def _split3(x):
    hi = x.astype(BF16)
    r1 = x - hi.astype(F32)
    mid = r1.astype(BF16)
    lo = (r1 - mid.astype(F32)).astype(BF16)
    return hi, mid, lo


def _dot_sel_rhs(x, sel):
    hi, mid, lo = _split3(x)
    return _dot(hi, sel) + _dot(mid, sel) + _dot(lo, sel)


def _dot_sel_lhs(sel, x):
    hi, mid, lo = _split3(x)
    return _dot(sel, hi) + _dot(sel, mid) + _dot(sel, lo)


def _dot3(a, b, dot=_dot):
    ah, am, al = _split3(a)
    bh, bm, bl = _split3(b)
    return (dot(ah, bh) + (dot(ah, bm) + dot(am, bh))
            + (dot(ah, bl) + dot(am, bm) + dot(al, bh)))


def _softplus(x):
    return jnp.maximum(x, 0.0) + jnp.log1p(jnp.exp(-jnp.abs(x)))


def _sigmoid(x):
    return 1.0 / (1.0 + jnp.exp(-x))


def _params(sem):
    return pltpu.CompilerParams(dimension_semantics=sem, vmem_limit_bytes=VMEM_LIMIT)


def _rglru_stages(p, first, cw_ref, cb_ref, wax_ref, bax_ref, lam_ref, ng_ref, ones_ref, o_ref, xs_ref, h_ref):
    tb = p.shape[0]
    w = LRU_WIDTH
    hist = 8
    xs_ref[0:hist, :] = jnp.where(first, 0.0, xs_ref[tb:tb + hist, :])
    gate_in = p[:, 0:w]
    xin = p[:, w:2 * w]
    xs_ref[hist:hist + tb, :] = xin
    xc = cb_ref[...] + cw_ref[CONV_WIDTH - 1:CONV_WIDTH, :] * xin
    for j in range(1, CONV_WIDTH):
        xc = xc + cw_ref[CONV_WIDTH - 1 - j:CONV_WIDTH - j, :] * xs_ref[hist - j:hist - j + tb, :]
    gates = _dot(xc.astype(BF16), wax_ref[...]) + bax_ref[...]
    yield
    gate_a = _sigmoid(gates[:, 0:w])
    gate_x = _sigmoid(gates[:, w:2 * w])
    log_a = -LRU_C * gate_a * _softplus(-lam_ref[...])
    a = jnp.exp(log_a)
    bb = jnp.sqrt(1.0 - jnp.exp(2.0 * log_a)) * (gate_x * xc)
    yield
    row = lax.broadcasted_iota(jnp.int32, (tb, 1), 0)
    d = 1
    while d < tb:
        keep = row >= d
        a_s = jnp.where(keep, pltpu.roll(a, d, axis=0), 1.0)
        b_s = jnp.where(keep, pltpu.roll(bb, d, axis=0), 0.0)
        bb = a * b_s + bb
        a = a * a_s
        d *= 2
        yield
    h = bb + a * jnp.where(first, 0.0, h_ref[0:1, :])
    h_ref[0:1, :] = h[tb - 1:tb, :]
    gelu = 0.5 * gate_in * (1.0 + jnp.tanh(0.7978845608028654 * (gate_in + 0.044715 * (gate_in * gate_in * gate_in))))
    y = gelu * h
    ms = _dot((y * y).astype(BF16), ones_ref[...]) * (1.0 / HEAD_DIM)
    o_ref[...] = y * lax.rsqrt(ms + RMS_EPS) * ng_ref[...]


def _inproj_kernel(x_ref, w_ref, wkt_ref, cw_ref, cb_ref, wax_ref, bax_ref, lam_ref, ng_ref, ones_ref,
                   o_r, o_yl, o_q, o_kt, o_v, xs_ref, h_ref, *, tiles_per_seq):
    xb = x_ref[...].astype(BF16)
    first = pl.program_id(0) % tiles_per_seq == 0
    p_l = _dot(xb, w_ref[:, RWKV_COLS:RWKV_COLS + LRU_COLS])
    lru = _rglru_stages(p_l, first, cw_ref, cb_ref, wax_ref, bax_ref, lam_ref, ng_ref, ones_ref, o_yl,
                        xs_ref, h_ref)

    def project(o_ref, col, c0, c1):
        o_ref[:, c0:c1] = _dot(xb, w_ref[:, col + c0:col + c1]).astype(o_ref.dtype)

    tk = o_kt.shape[2]

    def project_keys(r0, r1):
        kt = _dot_nt(wkt_ref[r0:r1, :], xb)
        for i in range(o_kt.shape[0]):
            o_kt[i, r0:r1, :] = kt[:, i * tk:(i + 1) * tk].astype(o_kt.dtype)

    work = []
    col = 0
    for o_ref, skip in ((o_r, LRU_COLS), (o_q, SBA_WIDTH), (o_v, 0)):
        width = o_ref.shape[1]
        for c0 in range(0, width, PROJ_CHUNK):
            work.append(functools.partial(project, o_ref, col, c0, min(c0 + PROJ_CHUNK, width)))
        col += width + skip
    for r0 in range(0, SBA_WIDTH, LANES):
        work.append(functools.partial(project_keys, r0, r0 + LANES))
    for step in work:
        step()
        next(lru, None)
    for _ in lru:
        pass


def _block_diag(wb):
    n, d, _ = wb.shape
    eye = jnp.eye(n, dtype=wb.dtype)
    return jnp.einsum('gij,gh->gihj', wb, eye).reshape(n * d, n * d)


def _inproj(x2, seq, w, conv_w, conv_b, wa, ba, wx, bx, lam, norm_g, tm=512, tk=SBA_KEY_TILE):
    t, d = x2.shape
    n = w.shape[1]
    tm = min(tm, seq)
    lw = LRU_WIDTH
    q0 = RWKV_COLS + LRU_COLS
    k0 = q0 + SBA_WIDTH
    col = jnp.arange(n)
    w_bf16 = (w * jnp.where((col >= q0) & (col < k0), LOG2E * HEAD_DIM ** -0.5, 1.0)).astype(BF16)
    wkt = w_bf16[:, k0:k0 + SBA_WIDTH].T
    wax = jnp.concatenate([_block_diag(wa), _block_diag(wx)], axis=1).astype(BF16)
    bax = jnp.concatenate([ba, bx]).reshape(1, 2 * lw)
    head = jnp.arange(lw) // HEAD_DIM
    ones = (head[:, None] == head[None, :]).astype(BF16)
    full = lambda shape: pl.BlockSpec(shape, lambda i: (0,) * len(shape))
    return pl.pallas_call(
        functools.partial(_inproj_kernel, tiles_per_seq=seq // tm),
        grid=(t // tm,),
        in_specs=[pl.BlockSpec((tm, d), lambda i: (i, 0)), full((d, n)), full((SBA_WIDTH, d)),
                  full((CONV_WIDTH, lw)), full((1, lw)), full((lw, 2 * lw)), full((1, 2 * lw)),
                  full((1, lw)), full((1, lw)), full((lw, lw))],
        out_specs=[pl.BlockSpec((tm, RWKV_COLS), lambda i: (i, 0)),
                   pl.BlockSpec((tm, lw), lambda i: (i, 0)),
                   pl.BlockSpec((tm, SBA_WIDTH), lambda i: (i, 0)),
                   pl.BlockSpec((tm // tk, SBA_WIDTH, tk), lambda i: (i, 0, 0)),
                   pl.BlockSpec((tm, SBA_WIDTH), lambda i: (i, 0))],
        out_shape=[jax.ShapeDtypeStruct((t, RWKV_COLS), F32),
                   jax.ShapeDtypeStruct((t, lw), F32),
                   jax.ShapeDtypeStruct((t, SBA_WIDTH), BF16),
                   jax.ShapeDtypeStruct((t // tk, SBA_WIDTH, tk), BF16),
                   jax.ShapeDtypeStruct((t, SBA_WIDTH), BF16)],
        scratch_shapes=[pltpu.VMEM((tm + 8, lw), F32), pltpu.VMEM((8, lw), F32)],
        compiler_params=_params(("arbitrary",)),
        name="inproj_rglru",
    )(x2, w_bf16, wkt, conv_w, conv_b.reshape(1, lw), wax, bax, lam.reshape(1, lw), norm_g.reshape(1, lw), ones)


def _rwkv_kernel(p_ref, mu_ref, w0_ref, a0_ref, kk_ref, ka_ref, rk_ref, lng_ref, lnb_ref,
                 wup_ref, aup_ref, gup_ref, ones_ref, o_ref, carry_ref, ht_ref, *, nb):
    c = RWKV_CHUNK
    w = RWKV_WIDTH
    rows = nb * c
    t = pl.program_id(1)

    @pl.when(t == 0)
    def _():
        carry_ref[...] = jnp.zeros_like(carry_ref)
        ht_ref[...] = jnp.zeros_like(ht_ref)

    p = p_ref[...].reshape(rows, RWKV_COLS)
    row = lax.broadcasted_iota(jnp.int32, (rows, 1), 0)
    prev = pltpu.roll(p, 1, axis=0)
    for n in range(nb):
        prev = jnp.where(row == n * c, carry_ref[n:n + 1, :], prev)
    for n in range(nb):
        carry_ref[n:n + 1, :] = p_ref[n, c - 1:c, :]
    ps = p + mu_ref[...] * (prev - p)

    r = ps[:, 0:w]
    k = ps[:, w:2 * w]
    v = ps[:, 2 * w:3 * w]
    lora_in = ps[:, 3 * w:3 * w + LANES]
    dg = ps[:, 3 * w + LANES:3 * w + 2 * LANES]

    w_log = -_softplus(-(w0_ref[...] + _dot(jnp.tanh(lora_in).astype(BF16), wup_ref[...]))) - 0.5
    logw = -jnp.exp(w_log)
    a = _sigmoid(a0_ref[...] + _dot(lora_in.astype(BF16), aup_ref[...]))
    g = _dot(_sigmoid(dg).astype(BF16), gup_ref[...])

    ones = ones_ref[...]

    def seg_sum(x):
        return jnp.concatenate([_dot(x[:, i:i + LANES].astype(BF16), ones) for i in range(0, w, LANES)], axis=1)

    kkr = k * kk_ref[...]
    kk = kkr / jnp.maximum(jnp.sqrt(seg_sum(kkr * kkr)), 1e-12)
    k2 = k * (1.0 + (a - 1.0) * ka_ref[...])
    b = kk * a

    grows = min(rows, MXU_WIDTH)
    ri = lax.broadcasted_iota(jnp.int32, (grows, grows), 0)
    ci = lax.broadcasted_iota(jnp.int32, (grows, grows), 1)
    tri = jnp.where(((ri // c) == (ci // c)) & (ci <= ri), 1.0, 0.0).astype(BF16)
    lw_hi = logw.astype(BF16)
    lw_lo = (logw - lw_hi.astype(F32)).astype(BF16)
    cum = jnp.concatenate([_dot(tri, lw_hi[i:i + grows]) + _dot(tri, lw_lo[i:i + grows])
                           for i in range(0, rows, grows)], axis=0)

    lane = lax.broadcasted_iota(jnp.int32, (1, LANES), 1)
    first = lane < HEAD_DIM

    def stack(x):
        return jnp.concatenate([jnp.where(first, x, 0.0), jnp.where(first, 0.0, x)], axis=0)

    ri2 = lax.broadcasted_iota(jnp.int32, (2 * c, 2 * c), 0)
    ci2 = lax.broadcasted_iota(jnp.int32, (2 * c, 2 * c), 1)
    same = (ri2 // c) == (ci2 // c)
    strict = same & (ci2 < ri2)
    incl = same & (ci2 <= ri2)
    eye = jnp.where(ri2 == ci2, 1.0, 0.0)

    def dotb(x, y):
        return _dot(x.astype(BF16), y.astype(BF16))

    cum_tot = jnp.concatenate(
        [jnp.broadcast_to(cum[n * c + c - 1:n * c + c, :], (c, w)) for n in range(nb)], axis=0)
    rt = r * jnp.exp(cum)
    at = -kk * jnp.exp(cum - logw)
    e_neg = jnp.exp(-cum)
    bt = b * e_neg
    kt = k2 * e_neg
    e_rem = jnp.exp(cum_tot - cum)
    bh = b * e_rem
    kh = k2 * e_rem
    e_tot = jnp.exp(cum_tot)

    chains = [(n, j) for n in range(nb) for j in range(w // LANES)]

    def cut(x, n, j):
        return x[n * c:(n + 1) * c, j * LANES:(j + 1) * LANES]

    ar = [jnp.concatenate([stack(cut(at, n, j)), stack(cut(rt, n, j))], axis=0).astype(BF16)
          for n, j in chains]
    bk = [jnp.concatenate([stack(cut(bt, n, j)), stack(cut(kt, n, j))], axis=0).astype(BF16)
          for n, j in chains]
    gram = [_dot_nt(x, y) for x, y in zip(ar, bk)]
    ht = [ht_ref[n, j] for n, j in chains]
    xh = [_dot_nt(x, h.astype(BF16)) for x, h in zip(ar, ht)]
    v_b = [stack(cut(v, n, j)).astype(BF16) for n, j in chains]
    a_ab = [jnp.where(strict, gm[0:2 * c, 0:2 * c], 0.0) for gm in gram]
    a_ak = [jnp.where(strict, gm[0:2 * c, 2 * c:4 * c], 0.0).astype(BF16) for gm in gram]
    a_r = [jnp.concatenate([jnp.where(incl, gm[2 * c:4 * c, 0:2 * c], 0.0),
                            jnp.where(incl, gm[2 * c:4 * c, 2 * c:4 * c], 0.0)], axis=1).astype(BF16)
           for gm in gram]
    rhs_u = [x[0:2 * c] + _dot(ak, vb) for x, ak, vb in zip(xh, a_ak, v_b)]

    pw = [x.astype(BF16) for x in a_ab]
    inv = [eye + x for x in a_ab]
    for _ in range(5):
        pw_f = [_dot(x, x) for x in pw]
        pw = [x.astype(BF16) for x in pw_f]
        inv = [x + _dot(x.astype(BF16), y) for x, y in zip(inv, pw)]

    u_d = [dotb(x, y) for x, y in zip(inv, rhs_u)]
    uv = [jnp.concatenate([u.astype(BF16), vb], axis=0) for u, vb in zip(u_d, v_b)]
    o_d = [x[2 * c:4 * c] + _dot(a, y) for x, a, y in zip(xh, a_r, uv)]
    bkh = [jnp.concatenate([stack(cut(bh, n, j)), stack(cut(kh, n, j))], axis=0).astype(BF16)
           for n, j in chains]
    upd = [_dot_tn(y, z) for y, z in zip(uv, bkh)]
    for (n, j), h, u in zip(chains, ht, upd):
        ht_ref[n, j] = h * e_tot[n * c:n * c + 1, j * LANES:(j + 1) * LANES] + u
    o_pair = [x[0:c] + x[c:2 * c] for x in o_d]
    npairs = w // LANES
    o = jnp.concatenate([jnp.concatenate(o_pair[n * npairs:(n + 1) * npairs], axis=1)
                         for n in range(nb)], axis=0)
    inv_n = 1.0 / HEAD_DIM
    d = o - seg_sum(o) * inv_n
    var = seg_sum(d * d) * inv_n
    on = d * lax.rsqrt(var + RWKV_GN_EPS) * lng_ref[...] + lnb_ref[...]
    bonus = seg_sum(r * k2 * rk_ref[...]) * v
    o_ref[...] = ((on + bonus) * g).reshape(nb, c, w)


def _rwkv(p_r, mu, w0, wup, a0, aup, gup, k_k, k_a, r_k, ln_g, ln_b, nb=8):
    bsz, seq, _ = p_r.shape
    nb = min(nb, bsz)
    c = RWKV_CHUNK
    w = RWKV_WIDTH
    zeros = jnp.zeros((DECAY_RANK, w), F32)
    wup_pad = jnp.concatenate([wup, zeros], axis=0).astype(BF16)
    aup_pad = jnp.concatenate([zeros, aup], axis=0).astype(BF16)
    head = jnp.arange(LANES) // HEAD_DIM
    ones = (head[:, None] == head[None, :]).astype(BF16)
    vec = lambda x: x.reshape(1, -1)
    full = lambda shape: pl.BlockSpec(shape, lambda b, t: (0,) * len(shape))
    return pl.pallas_call(
        functools.partial(_rwkv_kernel, nb=nb),
        grid=(bsz // nb, seq // c),
        in_specs=[pl.BlockSpec((nb, c, RWKV_COLS), lambda b, t: (b, t, 0)),
                  full((1, RWKV_COLS))] + [full((1, w))] * 7
                 + [full((LANES, w)), full((LANES, w)), full((GATE_RANK, w)), full((LANES, LANES))],
        out_specs=pl.BlockSpec((nb, c, w), lambda b, t: (b, t, 0)),
        out_shape=jax.ShapeDtypeStruct((bsz, seq, w), F32),
        scratch_shapes=[pltpu.VMEM((8, RWKV_COLS), F32),
                        pltpu.VMEM((nb, w // LANES, LANES, LANES), F32)],
        compiler_params=_params(("parallel", "arbitrary")),
        name="rwkv7",
    )(p_r, vec(mu), vec(w0), vec(a0), vec(k_k), vec(k_a), vec(r_k), vec(ln_g), vec(ln_b),
      wup_pad, aup_pad, gup.astype(BF16), ones)


def _sba_kernel(q_ref, kt_ref, v_ref, g_ref, cs_ref, ones_ref, o_ref, acc_ref, car_ref, *, tq, tk):
    qi = pl.program_id(1)
    nsub = tq // tk
    lane = lax.broadcasted_iota(jnp.int32, (1, LANES), 1)
    first = lane < HEAD_DIM
    cs = cs_ref[...]
    for pair in range(SBA_WIDTH // LANES):
        _sba_pair(slice(pair * LANES, (pair + 1) * LANES), qi, nsub, first, cs, q_ref, kt_ref, v_ref, g_ref,
                  ones_ref, o_ref, acc_ref, car_ref, tq, tk)


def _sba_pair(cols, qi, nsub, first, cs, q_ref, kt_ref, v_ref, g_ref, ones_ref, o_ref, acc_ref, car_ref, tq, tk):
    q = q_ref[0, :, cols]
    zero = jnp.zeros_like(q)
    qh = (jnp.where(first, q, zero), jnp.where(first, zero, q))
    acc_ref[...] = jnp.zeros_like(acc_ref)
    car_ref[...] = jnp.zeros_like(car_ref)

    def tile(j, r0, masked):
        kt = kt_ref[j, cols, :]
        vt = v_ref[0, pl.ds(pl.multiple_of(j * tk, tk), tk), cols]
        if masked:
            ti = lax.broadcasted_iota(jnp.int32, (tq - r0, tk), 0)
            si = lax.broadcasted_iota(jnp.int32, (tq - r0, tk), 1)
            mask = si < ti
        for h in range(2):
            z = _dot(qh[h][r0:], kt)
            neg_abs = lax.bitcast_convert_type(lax.bitcast_convert_type(z, jnp.int32) | SIGN_BIT, F32)
            nk = jnp.maximum(z, 0.0) + jnp.log2(1.0 + jnp.exp2(neg_abs))
            if masked:
                nk = jnp.where(mask, nk, 0.0)
            after = _dot(nk.astype(BF16), cs)
            car = car_ref[h, r0:, :]
            attn = jnp.exp2((z - nk) + after + car)
            if masked:
                attn = jnp.where(mask, attn, 0.0)
            acc_ref[h, r0:, :] += _dot(attn.astype(BF16), vt)
            car_ref[h, r0:, :] = car + (after[:, 0:1] - nk[:, 0:1])

    for d in range(nsub - 1, -1, -1):
        tile(qi * nsub + d, d * tk, True)

    def alive():
        return (jnp.max(car_ref[...]) >= SBA_DEAD_LOG2).astype(jnp.int32)

    def cond(state):
        i, live = state
        return (i < qi * nsub) & (live > 0)

    def body(state):
        i, _ = state
        tile(qi * nsub - 1 - i, 0, False)
        return i + 1, alive()

    lax.while_loop(cond, body, (jnp.int32(0), alive()))
    o = jnp.where(first, acc_ref[0], acc_ref[1])
    ms = _dot((o * o).astype(BF16), ones_ref[...]) * (1.0 / HEAD_DIM)
    o_ref[0, :, cols] = o * lax.rsqrt(ms + RMS_EPS) * g_ref[:, cols]


def _sba(q, kt, v, norm_g, tq=512):
    bsz, seq, _ = q.shape
    tk = kt.shape[2]
    tq = min(tq, seq)
    ji = jnp.arange(tk)
    cs = -(ji[:, None] > ji[None, :]).astype(BF16)
    head = jnp.arange(LANES) // HEAD_DIM
    ones = (head[:, None] == head[None, :]).astype(BF16)
    return pl.pallas_call(
        functools.partial(_sba_kernel, tq=tq, tk=tk),
        grid=(bsz, seq // tq),
        in_specs=[pl.BlockSpec((1, tq, SBA_WIDTH), lambda b, i: (b, i, 0)),
                  pl.BlockSpec((seq // tk, SBA_WIDTH, tk), lambda b, i: (b, 0, 0)),
                  pl.BlockSpec((1, seq, SBA_WIDTH), lambda b, i: (b, 0, 0)),
                  pl.BlockSpec((1, SBA_WIDTH), lambda b, i: (0, 0)),
                  pl.BlockSpec((tk, tk), lambda b, i: (0, 0)),
                  pl.BlockSpec((LANES, LANES), lambda b, i: (0, 0))],
        out_specs=pl.BlockSpec((1, tq, SBA_WIDTH), lambda b, i: (b, i, 0)),
        out_shape=jax.ShapeDtypeStruct((bsz, seq, SBA_WIDTH), F32),
        scratch_shapes=[pltpu.VMEM((2, tq, LANES), F32), pltpu.VMEM((2, tq, 1), F32)],
        compiler_params=_params(("parallel", "arbitrary")),
        name="stickbreak",
    )(q, kt, v, norm_g.reshape(1, SBA_WIDTH), cs, ones)


def _layer_norm(h, g, b):
    mu = jnp.mean(h, axis=-1, keepdims=True)
    d = h - mu
    var = jnp.mean(d * d, axis=-1, keepdims=True)
    return d * lax.rsqrt(var + LN_EPS) * g + b


def _route(scores):
    s = [scores[e:e + 1, :] for e in range(N_EXPERTS)]
    top2 = []
    gsum = []
    for grp in range(N_GROUPS):
        mem = list(range(grp * EXPERTS_PER_GROUP, (grp + 1) * EXPERTS_PER_GROUP))
        tot = None
        for e in mem:
            rank = None
            for o in mem:
                if o == e:
                    continue
                ahead = (s[o] > s[e]) | ((s[o] == s[e]) if o < e else False)
                ahead = jnp.where(ahead, 1.0, 0.0)
                rank = ahead if rank is None else rank + ahead
            sel = rank < 1.5
            top2.append(sel)
            part = jnp.where(sel, s[e], 0.0)
            tot = part if tot is None else tot + part
        gsum.append(tot)
    best = gsum[0]
    best_g = jnp.zeros_like(best)
    for grp in range(1, N_GROUPS):
        upd = gsum[grp] > best
        best = jnp.where(upd, gsum[grp], best)
        best_g = jnp.where(upd, float(grp), best_g)
    rows = []
    for k in range(EXPERTS_PER_GROUP):
        acc = None
        for grp in range(N_GROUPS):
            e = grp * EXPERTS_PER_GROUP + k
            part = jnp.where((best_g == float(grp)) & top2[e], s[e] / gsum[grp], 0.0)
            acc = part if acc is None else acc + part
        rows.append(acc)
    return jnp.concatenate(rows, axis=0), best_g


def _outproj_kernel(yr_ref, yl_ref, ys_ref, x_ref, w_ref, g_ref, b_ref, rw_ref, rb_ref,
                    x1_ref, g4_ref, grp_ref, cnt_ref):
    tm = x_ref.shape[0]
    part = tm // OUTPROJ_PARTS
    spans = [(i * part, (i + 1) * part) for i in range(OUTPROJ_PARTS)]
    r0, r1 = RWKV_WIDTH, RWKV_WIDTH + LRU_WIDTH
    x1 = []
    for a, b in spans:
        mix = (_dot(yr_ref[a:b, :].astype(BF16), w_ref[0:r0, :])
               + _dot(yl_ref[a:b, :].astype(BF16), w_ref[r0:r1, :])
               + _dot(ys_ref[a:b, :].astype(BF16), w_ref[r1:, :]))
        x1.append(_layer_norm(ALPHA * x_ref[a:b, :] + mix, g_ref[...], b_ref[...]))
        x1_ref[a:b, :] = x1[-1]
    rw_hi = rw_ref[...].astype(BF16)
    rw_lo = (rw_ref[...] - rw_hi.astype(F32)).astype(BF16)
    logits = []
    for v in x1:
        hi = v.astype(BF16)
        lo = (v - hi.astype(F32)).astype(BF16)
        logits.append(_dot_nt(rw_hi, hi) + (_dot_nt(rw_hi, lo) + _dot_nt(rw_lo, hi)) + rb_ref[...])
    lane = lax.broadcasted_iota(jnp.int32, (1, LANES), 1)
    cnt = jnp.zeros((1, LANES), F32)
    for (a, b), lg in zip(spans, logits):
        ex = jnp.exp(lg - jnp.max(lg, axis=0, keepdims=True))
        gates, best_g = _route(ex / jnp.sum(ex, axis=0, keepdims=True))
        g4_ref[:, a:b] = gates
        grp_ref[:, a:b] = best_g.astype(jnp.int32)
        for grp in range(N_GROUPS):
            n_grp = jnp.sum(jnp.where(best_g == float(grp), 1.0, 0.0), axis=1, keepdims=True)
            cnt = cnt + jnp.where(lane == grp, n_grp, 0.0)
    cnt_ref[0] = cnt.astype(jnp.int32)


def _outproj(y_r, y_l, y_s, x2, w_bf16, ln_g, ln_b, router_wt, router_b, tm=ROUTE_TILE):
    t, d = x2.shape
    tm = min(tm, t)
    row = lambda width: pl.BlockSpec((tm, width), lambda i: (i, 0))
    full = lambda shape: pl.BlockSpec(shape, lambda i: (0,) * len(shape))
    return pl.pallas_call(
        _outproj_kernel,
        grid=(t // tm,),
        in_specs=[row(RWKV_WIDTH), row(LRU_WIDTH), row(SBA_WIDTH), row(d), full((d, d)),
                  full((1, d)), full((1, d)), full((N_EXPERTS, d)), full((N_EXPERTS, 1))],
        out_specs=[row(d), pl.BlockSpec((EXPERTS_PER_GROUP, tm), lambda i: (0, i)),
                   pl.BlockSpec((1, tm), lambda i: (0, i)),
                   pl.BlockSpec((1, 1, LANES), lambda i: (i, 0, 0))],
        out_shape=[jax.ShapeDtypeStruct((t, d), F32),
                   jax.ShapeDtypeStruct((EXPERTS_PER_GROUP, t), F32),
                   jax.ShapeDtypeStruct((1, t), jnp.int32),
                   jax.ShapeDtypeStruct((t // tm, 1, LANES), jnp.int32)],
        compiler_params=_params(("parallel",)),
        name="outproj_ln_router",
    )(y_r, y_l, y_s, x2, w_bf16, ln_g.reshape(1, d), ln_b.reshape(1, d), router_wt,
      router_b.reshape(N_EXPERTS, 1))


def _moe_kernel(cnt_ref, x_ref, grp_ref, g4_ref, wg_ref, wu_ref, wd_ref, g_ref, b_ref, o_ref,
                xs_ref, gs_ref, ys_ref, dest_ref, *, tm, rb, nsub):
    i = pl.program_id(0)
    e = pl.program_id(1)
    cap = xs_ref.shape[0]
    chunks = tm // LANES
    n = [sum(cnt_ref[i * nsub + s, grp] for s in range(nsub)) for grp in range(N_GROUPS)]
    start = [0]
    for grp in range(1, N_GROUPS):
        start.append(start[-1] + n[grp - 1])

    @pl.when(e == 0)
    def _():
        grp = grp_ref[...]
        onehot = jnp.concatenate([jnp.where(grp == k, 1.0, 0.0) for k in range(N_GROUPS)], axis=0)
        li = lax.broadcasted_iota(jnp.int32, (LANES, LANES), 0)
        lj = lax.broadcasted_iota(jnp.int32, (LANES, LANES), 1)
        before = jnp.where(li < lj, 1.0, 0.0).astype(BF16)
        within = _dot(onehot.astype(BF16), before)
        tot = jnp.broadcast_to(jnp.sum(onehot, axis=1, keepdims=True), onehot.shape)
        ri = lax.broadcasted_iota(jnp.int32, (N_GROUPS * chunks, N_GROUPS * chunks), 0)
        ci = lax.broadcasted_iota(jnp.int32, (N_GROUPS * chunks, N_GROUPS * chunks), 1)
        earlier = jnp.where(((ri // chunks) == (ci // chunks)) & (ci < ri), 1.0, 0.0).astype(BF16)
        rank = within + _dot(earlier, tot.astype(BF16))
        dest = jnp.zeros((chunks, LANES), F32)
        for k in range(N_GROUPS):
            dest = jnp.where(grp == k, rank[k * chunks:(k + 1) * chunks] + jnp.asarray(start[k]).astype(F32), dest)
        dest_ref[...] = dest
        dest_i = dest.astype(jnp.int32)
        r_iota = lax.broadcasted_iota(jnp.int32, (tm, LANES), 0)
        place = jnp.concatenate(
            [jnp.where(r_iota == dest_i[c:c + 1, :], 1.0, 0.0).astype(BF16) for c in range(chunks)],
            axis=1)
        xs_ref[0:tm, :] = _dot(place, x_ref[...].astype(BF16)).astype(BF16)
        xs_ref[tm:cap, :] = jnp.zeros((cap - tm, xs_ref.shape[1]), BF16)
        pieces = [p.astype(F32) for p in _split3(g4_ref[...])]
        gates_t = jnp.transpose(jnp.concatenate(
            pieces + [jnp.zeros((LANES - 3 * EXPERTS_PER_GROUP, tm), F32)], axis=0))
        gs_ref[0:tm, :] = _dot(place, gates_t.astype(BF16))
        gs_ref[tm:cap, :] = jnp.zeros((cap - tm, LANES), F32)
        ys_ref[...] = jnp.zeros_like(ys_ref)

    e0 = e * MOE_EXPERTS_PER_STEP
    grp_e = e0 // EXPERTS_PER_GROUP
    n_e = n[0]
    s_e = start[0]
    for grp in range(1, N_GROUPS):
        n_e = jnp.where(grp_e == grp, n[grp], n_e)
        s_e = jnp.where(grp_e == grp, start[grp], s_e)
    lane = lax.broadcasted_iota(jnp.int32, (1, LANES), 1)
    s_al = s_e // MOE_ALIGN * MOE_ALIGN

    def block(b, carry):
        r0 = pl.multiple_of(s_al + b * rb, MOE_ALIGN)
        xb = xs_ref[pl.ds(r0, rb), :]
        gates = gs_ref[pl.ds(r0, rb), :]
        rid = r0 + lax.broadcasted_iota(jnp.int32, (rb, 1), 0)
        inside = (rid >= s_e) & (rid < s_e + n_e)
        acc = None
        for j in range(MOE_EXPERTS_PER_STEP):
            k_e = e0 % EXPERTS_PER_GROUP + j
            mine = ((lane % EXPERTS_PER_GROUP) == k_e) & (lane < 3 * EXPERTS_PER_GROUP)
            gate = jnp.where(inside, jnp.sum(jnp.where(mine, gates, 0.0), axis=1, keepdims=True), 0.0)
            hg = _dot(xb, wg_ref[j])
            hu = _dot(xb, wu_ref[j])
            h = hg * _sigmoid(hg) * hu * gate
            y = _dot(h.astype(BF16), wd_ref[j])
            acc = y if acc is None else acc + y
        rows = pl.ds(r0, rb)
        ys_ref[rows, :] = jnp.where(inside, acc.astype(BF16), ys_ref[rows, :])
        return carry

    lax.fori_loop(0, jnp.where(n_e > 0, (s_e + n_e - s_al + rb - 1) // rb, 0), block, 0)

    @pl.when(e == pl.num_programs(1) - 1)
    def _():
        dest = dest_ref[...]
        dest_t = jnp.transpose(jnp.concatenate([dest, jnp.zeros((LANES - chunks, LANES), F32)], axis=0))
        dest_t = dest_t.astype(jnp.int32)
        c_iota = lax.broadcasted_iota(jnp.int32, (LANES, tm), 1)
        ys = ys_ref[0:tm, :]
        for c in range(chunks):
            rows = slice(c * LANES, (c + 1) * LANES)
            unplace = jnp.where(c_iota == dest_t[:, c:c + 1], 1.0, 0.0).astype(BF16)
            o_ref[rows, :] = _layer_norm(ALPHA * x_ref[rows, :] + _dot(unplace, ys), g_ref[...], b_ref[...])


def _moe(x1, g4, grp, cnt, wg, wu, wd, ln_g, ln_b, tm=MOE_TILE, rb=MOE_ROWS):
    t, d = x1.shape
    de = wg.shape[2]
    tm = min(tm, t)
    nsub = cnt.shape[0] * tm // t
    cap = tm + rb
    eps = MOE_EXPERTS_PER_STEP
    grid_spec = pltpu.PrefetchScalarGridSpec(
        num_scalar_prefetch=1,
        grid=(t // tm, N_EXPERTS // eps),
        in_specs=[pl.BlockSpec((tm, d), lambda i, e, c: (i, 0)),
                  pl.BlockSpec((tm // LANES, LANES), lambda i, e, c: (i, 0)),
                  pl.BlockSpec((EXPERTS_PER_GROUP, tm), lambda i, e, c: (0, i)),
                  pl.BlockSpec((eps, d, de), lambda i, e, c: (e, 0, 0)),
                  pl.BlockSpec((eps, d, de), lambda i, e, c: (e, 0, 0)),
                  pl.BlockSpec((eps, de, d), lambda i, e, c: (e, 0, 0)),
                  pl.BlockSpec((1, d), lambda i, e, c: (0, 0)),
                  pl.BlockSpec((1, d), lambda i, e, c: (0, 0))],
        out_specs=pl.BlockSpec((tm, d), lambda i, e, c: (i, 0)),
        scratch_shapes=[pltpu.VMEM((cap, d), BF16), pltpu.VMEM((cap, LANES), F32),
                        pltpu.VMEM((cap, d), BF16), pltpu.VMEM((tm // LANES, LANES), F32)])
    return pl.pallas_call(
        functools.partial(_moe_kernel, tm=tm, rb=rb, nsub=nsub),
        grid_spec=grid_spec,
        out_shape=jax.ShapeDtypeStruct((t, d), F32),
        compiler_params=_params(("parallel", "arbitrary")),
        name="moe_ln",
    )(cnt.reshape(cnt.shape[0], LANES)[:, :N_GROUPS], x1, grp.reshape(t // LANES, LANES), g4,
      wg, wu, wd, ln_g.reshape(1, d), ln_b.reshape(1, d))


def kernel(x, w_in, shift_mu, rwkv_w0, rwkv_w_up, rwkv_a0, rwkv_a_up, rwkv_g_up, rwkv_k_k, rwkv_k_a, rwkv_r_k, rwkv_ln_g, rwkv_ln_b, lru_conv_w, lru_conv_b, lru_wa, lru_ba, lru_wx, lru_bx, lru_lambda, lru_norm_g, sba_norm_g, w_out, ln1_g, ln1_b, ln2_g, ln2_b, router_w, router_b, exp_w_gate, exp_w_up, exp_w_down):
    bsz, seq, d = x.shape
    t = bsz * seq
    x2 = x.reshape(t, d)
    router_wt = router_w.T
    for l in range(w_in.shape[0]):
        p_r, y_l, p_q, p_kt, p_v = _inproj(x2, seq, w_in[l], lru_conv_w[l], lru_conv_b[l], lru_wa[l], lru_ba[l],
                                            lru_wx[l], lru_bx[l], lru_lambda[l], lru_norm_g[l])
        y_r = _rwkv(p_r.reshape(bsz, seq, RWKV_COLS), shift_mu[l], rwkv_w0[l], rwkv_w_up[l],
                    rwkv_a0[l], rwkv_a_up[l], rwkv_g_up[l], rwkv_k_k[l], rwkv_k_a[l],
                    rwkv_r_k[l], rwkv_ln_g[l], rwkv_ln_b[l])
        y_s = _sba(p_q.reshape(bsz, seq, SBA_WIDTH), p_kt, p_v.reshape(bsz, seq, SBA_WIDTH), sba_norm_g[l])
        x1, g4, grp, cnt = _outproj(y_r.reshape(t, RWKV_WIDTH), y_l,
                                    y_s.reshape(t, SBA_WIDTH), x2, w_out[l].astype(BF16),
                                    ln1_g[l], ln1_b[l], router_wt, router_b)
        x2 = _moe(x1, g4, grp, cnt, exp_w_gate[l].astype(BF16), exp_w_up[l].astype(BF16),
                  exp_w_down[l].astype(BF16), ln2_g[l], ln2_b[l])
    return x2.reshape(bsz, seq, d)
```

```python
import functools

import jax
import jax.numpy as jnp
from jax import lax
from jax.experimental import pallas as pl
from jax.experimental.pallas import tpu as pltpu

F32 = jnp.float32
BF16 = jnp.bfloat16

HEAD_DIM = 64
RWKV_WIDTH = 384
LRU_WIDTH = 256
SBA_WIDTH = 384
DECAY_RANK = 64
ICLR_RANK = 64
GATE_RANK = 128
RWKV_COLS = 3 * RWKV_WIDTH + DECAY_RANK + ICLR_RANK + GATE_RANK
LRU_COLS = 2 * LRU_WIDTH
RWKV_GN_EPS = 64e-5
CONV_WIDTH = 4
LRU_C = 8.0
N_EXPERTS = 16
N_GROUPS = 4
EXPERTS_PER_GROUP = 4
DEPTH = 4
ALPHA = (2.0 * DEPTH) ** 0.25
LN_EPS = 1e-5
RMS_EPS = 1e-6
LOG2E = 1.4426950408889634
SQRT_2_OVER_PI = 0.7978845608028654
SIGN_BIT = -2147483648
SBA_DEAD_LOG2 = -160.0

LANES = 128
MXU_WIDTH = 256
RWKV_CHUNK = 64
SBA_KEY_TILE = 256
ROUTE_TILE = 512
OUTPROJ_PARTS = 4
PROJ_CHUNK = 256
MOE_TILE = 1024
MOE_ROWS = 288
MOE_ALIGN = 16
MOE_EXPERTS_PER_STEP = 4
VMEM_LIMIT = 52 * 1024 * 1024


def _dot(a, b):
    return jnp.dot(a, b, preferred_element_type=F32)


def _dot_nt(a, b):
    return lax.dot_general(a, b, (((1,), (1,)), ((), ())), preferred_element_type=F32)


def _dot_tn(a, b):
    return lax.dot_general(a, b, (((0,), (0,)), ((), ())), preferred_element_type=F32)


def _split3(x):
    hi = x.astype(BF16)
    r1 = x - hi.astype(F32)
    mid = r1.astype(BF16)
    lo = (r1 - mid.astype(F32)).astype(BF16)
    return hi, mid, lo


def _softplus(x):
    return jnp.maximum(x, 0.0) + jnp.log1p(jnp.exp(-jnp.abs(x)))


def _sigmoid(x):
    return 1.0 / (1.0 + jnp.exp(-x))


def _params(sem):
    return pltpu.CompilerParams(dimension_semantics=sem, vmem_limit_bytes=VMEM_LIMIT)


def _rglru_stages(p, first, cw_ref, cb_ref, wax_ref, bax_ref, lam_ref, ng_ref, ones_ref, o_ref, xs_ref, h_ref):
    tb = p.shape[0]
    w = LRU_WIDTH
    hist = 8
    xs_ref[0:hist, :] = jnp.where(first, 0.0, xs_ref[tb:tb + hist, :])
    gate_in = p[:, 0:w]
    xin = p[:, w:2 * w]
    xs_ref[hist:hist + tb, :] = xin
    xc = cb_ref[...] + cw_ref[CONV_WIDTH - 1:CONV_WIDTH, :] * xin
    for j in range(1, CONV_WIDTH):
        xc = xc + cw_ref[CONV_WIDTH - 1 - j:CONV_WIDTH - j, :] * xs_ref[hist - j:hist - j + tb, :]
    gates = _dot(xc.astype(BF16), wax_ref[...]) + bax_ref[...]
    yield
    gate_a = _sigmoid(gates[:, 0:w])
    gate_x = _sigmoid(gates[:, w:2 * w])
    log_a = -LRU_C * gate_a * _softplus(-lam_ref[...])
    a = jnp.exp(log_a)
    bb = jnp.sqrt(1.0 - jnp.exp(2.0 * log_a)) * (gate_x * xc)
    yield
    row = lax.broadcasted_iota(jnp.int32, (tb, 1), 0)
    d = 1
    while d < tb:
        keep = row >= d
        a_s = jnp.where(keep, pltpu.roll(a, d, axis=0), 1.0)
        b_s = jnp.where(keep, pltpu.roll(bb, d, axis=0), 0.0)
        bb = a * b_s + bb
        a = a * a_s
        d *= 2
        yield
    h = bb + a * jnp.where(first, 0.0, h_ref[0:1, :])
    h_ref[0:1, :] = h[tb - 1:tb, :]
    gelu = 0.5 * gate_in * (1.0 + jnp.tanh(SQRT_2_OVER_PI * (gate_in + 0.044715 * (gate_in * gate_in * gate_in))))
    y = gelu * h
    ms = _dot((y * y).astype(BF16), ones_ref[...]) * (1.0 / HEAD_DIM)
    o_ref[...] = y * lax.rsqrt(ms + RMS_EPS) * ng_ref[...]


def _inproj_kernel(x_ref, w_ref, wkt_ref, cw_ref, cb_ref, wax_ref, bax_ref, lam_ref, ng_ref, ones_ref,
                   o_r, o_yl, o_q, o_kt, o_v, xs_ref, h_ref, *, tiles_per_seq):
    xb = x_ref[...].astype(BF16)
    first = pl.program_id(0) % tiles_per_seq == 0
    p_l = _dot(xb, w_ref[:, RWKV_COLS:RWKV_COLS + LRU_COLS])
    lru = _rglru_stages(p_l, first, cw_ref, cb_ref, wax_ref, bax_ref, lam_ref, ng_ref, ones_ref, o_yl,
                        xs_ref, h_ref)

    def project(o_ref, col, c0, c1):
        o_ref[:, c0:c1] = _dot(xb, w_ref[:, col + c0:col + c1]).astype(o_ref.dtype)

    tk = o_kt.shape[2]

    def project_keys(r0, r1):
        kt = _dot_nt(wkt_ref[r0:r1, :], xb)
        for i in range(o_kt.shape[0]):
            o_kt[i, r0:r1, :] = kt[:, i * tk:(i + 1) * tk].astype(o_kt.dtype)

    work = []
    col = 0
    for o_ref, skip in ((o_r, LRU_COLS), (o_q, SBA_WIDTH), (o_v, 0)):
        width = o_ref.shape[1]
        for c0 in range(0, width, PROJ_CHUNK):
            work.append(functools.partial(project, o_ref, col, c0, min(c0 + PROJ_CHUNK, width)))
        col += width + skip
    for r0 in range(0, SBA_WIDTH, LANES):
        work.append(functools.partial(project_keys, r0, r0 + LANES))
    for step in work:
        step()
        next(lru, None)
    for _ in lru:
        pass


def _block_diag(wb):
    n, d, _ = wb.shape
    eye = jnp.eye(n, dtype=wb.dtype)
    return jnp.einsum('gij,gh->gihj', wb, eye).reshape(n * d, n * d)


def _inproj(x2, seq, w, conv_w, conv_b, wa, ba, wx, bx, lam, norm_g, tm=512, tk=SBA_KEY_TILE):
    t, d = x2.shape
    n = w.shape[1]
    tm = min(tm, seq)
    lw = LRU_WIDTH
    q0 = RWKV_COLS + LRU_COLS
    k0 = q0 + SBA_WIDTH
    col = jnp.arange(n)
    w_bf16 = (w * jnp.where((col >= q0) & (col < k0), LOG2E * HEAD_DIM ** -0.5, 1.0)).astype(BF16)
    wkt = w_bf16[:, k0:k0 + SBA_WIDTH].T
    wax = jnp.concatenate([_block_diag(wa), _block_diag(wx)], axis=1).astype(BF16)
    bax = jnp.concatenate([ba, bx]).reshape(1, 2 * lw)
    head = jnp.arange(lw) // HEAD_DIM
    ones = (head[:, None] == head[None, :]).astype(BF16)
    full = lambda shape: pl.BlockSpec(shape, lambda i: (0,) * len(shape))
    return pl.pallas_call(
        functools.partial(_inproj_kernel, tiles_per_seq=seq // tm),
        grid=(t // tm,),
        in_specs=[pl.BlockSpec((tm, d), lambda i: (i, 0)), full((d, n)), full((SBA_WIDTH, d)),
                  full((CONV_WIDTH, lw)), full((1, lw)), full((lw, 2 * lw)), full((1, 2 * lw)),
                  full((1, lw)), full((1, lw)), full((lw, lw))],
        out_specs=[pl.BlockSpec((tm, RWKV_COLS), lambda i: (i, 0)),
                   pl.BlockSpec((tm, lw), lambda i: (i, 0)),
                   pl.BlockSpec((tm, SBA_WIDTH), lambda i: (i, 0)),
                   pl.BlockSpec((tm // tk, SBA_WIDTH, tk), lambda i: (i, 0, 0)),
                   pl.BlockSpec((tm, SBA_WIDTH), lambda i: (i, 0))],
        out_shape=[jax.ShapeDtypeStruct((t, RWKV_COLS), F32),
                   jax.ShapeDtypeStruct((t, lw), F32),
                   jax.ShapeDtypeStruct((t, SBA_WIDTH), BF16),
                   jax.ShapeDtypeStruct((t // tk, SBA_WIDTH, tk), BF16),
                   jax.ShapeDtypeStruct((t, SBA_WIDTH), BF16)],
        scratch_shapes=[pltpu.VMEM((tm + 8, lw), F32), pltpu.VMEM((8, lw), F32)],
        compiler_params=_params(("arbitrary",)),
        name="inproj_rglru",
    )(x2, w_bf16, wkt, conv_w, conv_b.reshape(1, lw), wax, bax, lam.reshape(1, lw), norm_g.reshape(1, lw), ones)


def _rwkv_kernel(p_ref, mu_ref, w0_ref, a0_ref, kk_ref, ka_ref, rk_ref, lng_ref, lnb_ref,
                 wup_ref, aup_ref, gup_ref, ones_ref, o_ref, carry_ref, ht_ref, *, nb):
    c = RWKV_CHUNK
    w = RWKV_WIDTH
    rows = nb * c
    t = pl.program_id(1)

    @pl.when(t == 0)
    def _():
        carry_ref[...] = jnp.zeros_like(carry_ref)
        ht_ref[...] = jnp.zeros_like(ht_ref)

    p = p_ref[...].reshape(rows, RWKV_COLS)
    row = lax.broadcasted_iota(jnp.int32, (rows, 1), 0)
    prev = pltpu.roll(p, 1, axis=0)
    for n in range(nb):
        prev = jnp.where(row == n * c, carry_ref[n:n + 1, :], prev)
    for n in range(nb):
        carry_ref[n:n + 1, :] = p_ref[n, c - 1:c, :]
    ps = p + mu_ref[...] * (prev - p)

    r = ps[:, 0:w]
    k = ps[:, w:2 * w]
    v = ps[:, 2 * w:3 * w]
    lora_in = ps[:, 3 * w:3 * w + LANES]
    dg = ps[:, 3 * w + LANES:3 * w + 2 * LANES]

    w_log = -_softplus(-(w0_ref[...] + _dot(jnp.tanh(lora_in).astype(BF16), wup_ref[...]))) - 0.5
    logw = -jnp.exp(w_log)
    a = _sigmoid(a0_ref[...] + _dot(lora_in.astype(BF16), aup_ref[...]))
    g = _dot(_sigmoid(dg).astype(BF16), gup_ref[...])

    ones = ones_ref[...]

    def seg_sum(x):
        return jnp.concatenate([_dot(x[:, i:i + LANES].astype(BF16), ones) for i in range(0, w, LANES)], axis=1)

    kkr = k * kk_ref[...]
    kk = kkr / jnp.maximum(jnp.sqrt(seg_sum(kkr * kkr)), 1e-12)
    k2 = k * (1.0 + (a - 1.0) * ka_ref[...])
    b = kk * a

    grows = min(rows, MXU_WIDTH)
    ri = lax.broadcasted_iota(jnp.int32, (grows, grows), 0)
    ci = lax.broadcasted_iota(jnp.int32, (grows, grows), 1)
    tri = jnp.where(((ri // c) == (ci // c)) & (ci <= ri), 1.0, 0.0).astype(BF16)
    lw_hi = logw.astype(BF16)
    lw_lo = (logw - lw_hi.astype(F32)).astype(BF16)
    cum = jnp.concatenate([_dot(tri, lw_hi[i:i + grows]) + _dot(tri, lw_lo[i:i + grows])
                           for i in range(0, rows, grows)], axis=0)

    lane = lax.broadcasted_iota(jnp.int32, (1, LANES), 1)
    first = lane < HEAD_DIM

    def stack(x):
        return jnp.concatenate([jnp.where(first, x, 0.0), jnp.where(first, 0.0, x)], axis=0)

    ri2 = lax.broadcasted_iota(jnp.int32, (2 * c, 2 * c), 0)
    ci2 = lax.broadcasted_iota(jnp.int32, (2 * c, 2 * c), 1)
    same = (ri2 // c) == (ci2 // c)
    strict = same & (ci2 < ri2)
    incl = same & (ci2 <= ri2)
    eye = jnp.where(ri2 == ci2, 1.0, 0.0)

    def dotb(x, y):
        return _dot(x.astype(BF16), y.astype(BF16))

    cum_tot = jnp.concatenate(
        [jnp.broadcast_to(cum[n * c + c - 1:n * c + c, :], (c, w)) for n in range(nb)], axis=0)
    rt = r * jnp.exp(cum)
    at = -kk * jnp.exp(cum - logw)
    e_neg = jnp.exp(-cum)
    bt = b * e_neg
    kt = k2 * e_neg
    e_rem = jnp.exp(cum_tot - cum)
    bh = b * e_rem
    kh = k2 * e_rem
    e_tot = jnp.exp(cum_tot)

    chains = [(n, j) for n in range(nb) for j in range(w // LANES)]

    def cut(x, n, j):
        return x[n * c:(n + 1) * c, j * LANES:(j + 1) * LANES]

    ar = [jnp.concatenate([stack(cut(at, n, j)), stack(cut(rt, n, j))], axis=0).astype(BF16)
          for n, j in chains]
    bk = [jnp.concatenate([stack(cut(bt, n, j)), stack(cut(kt, n, j))], axis=0).astype(BF16)
          for n, j in chains]
    gram = [_dot_nt(x, y) for x, y in zip(ar, bk)]
    ht = [ht_ref[n, j] for n, j in chains]
    xh = [_dot_nt(x, h.astype(BF16)) for x, h in zip(ar, ht)]
    v_b = [stack(cut(v, n, j)).astype(BF16) for n, j in chains]
    a_ab = [jnp.where(strict, gm[0:2 * c, 0:2 * c], 0.0) for gm in gram]
    a_ak = [jnp.where(strict, gm[0:2 * c, 2 * c:4 * c], 0.0).astype(BF16) for gm in gram]
    a_r = [jnp.concatenate([jnp.where(incl, gm[2 * c:4 * c, 0:2 * c], 0.0),
                            jnp.where(incl, gm[2 * c:4 * c, 2 * c:4 * c], 0.0)], axis=1).astype(BF16)
           for gm in gram]
    rhs_u = [x[0:2 * c] + _dot(ak, vb) for x, ak, vb in zip(xh, a_ak, v_b)]

    pw = [x.astype(BF16) for x in a_ab]
    inv = [eye + x for x in a_ab]
    for _ in range(5):
        pw_f = [_dot(x, x) for x in pw]
        pw = [x.astype(BF16) for x in pw_f]
        inv = [x + _dot(x.astype(BF16), y) for x, y in zip(inv, pw)]

    u_d = [dotb(x, y) for x, y in zip(inv, rhs_u)]
    uv = [jnp.concatenate([u.astype(BF16), vb], axis=0) for u, vb in zip(u_d, v_b)]
    o_d = [x[2 * c:4 * c] + _dot(a, y) for x, a, y in zip(xh, a_r, uv)]
    bkh = [jnp.concatenate([stack(cut(bh, n, j)), stack(cut(kh, n, j))], axis=0).astype(BF16)
           for n, j in chains]
    upd = [_dot_tn(y, z) for y, z in zip(uv, bkh)]
    for (n, j), h, u in zip(chains, ht, upd):
        ht_ref[n, j] = h * e_tot[n * c:n * c + 1, j * LANES:(j + 1) * LANES] + u
    o_pair = [x[0:c] + x[c:2 * c] for x in o_d]
    npairs = w // LANES
    o = jnp.concatenate([jnp.concatenate(o_pair[n * npairs:(n + 1) * npairs], axis=1)
                         for n in range(nb)], axis=0)
    inv_n = 1.0 / HEAD_DIM
    d = o - seg_sum(o) * inv_n
    var = seg_sum(d * d) * inv_n
    on = d * lax.rsqrt(var + RWKV_GN_EPS) * lng_ref[...] + lnb_ref[...]
    bonus = seg_sum(r * k2 * rk_ref[...]) * v
    o_ref[...] = ((on + bonus) * g).reshape(nb, c, w)


def _rwkv(p_r, mu, w0, wup, a0, aup, gup, k_k, k_a, r_k, ln_g, ln_b, nb=8):
    bsz, seq, _ = p_r.shape
    nb = min(nb, bsz)
    c = RWKV_CHUNK
    w = RWKV_WIDTH
    zeros = jnp.zeros((DECAY_RANK, w), F32)
    wup_pad = jnp.concatenate([wup, zeros], axis=0).astype(BF16)
    aup_pad = jnp.concatenate([zeros, aup], axis=0).astype(BF16)
    head = jnp.arange(LANES) // HEAD_DIM
    ones = (head[:, None] == head[None, :]).astype(BF16)
    vec = lambda x: x.reshape(1, -1)
    full = lambda shape: pl.BlockSpec(shape, lambda b, t: (0,) * len(shape))
    return pl.pallas_call(
        functools.partial(_rwkv_kernel, nb=nb),
        grid=(bsz // nb, seq // c),
        in_specs=[pl.BlockSpec((nb, c, RWKV_COLS), lambda b, t: (b, t, 0)),
                  full((1, RWKV_COLS))] + [full((1, w))] * 7
                 + [full((LANES, w)), full((LANES, w)), full((GATE_RANK, w)), full((LANES, LANES))],
        out_specs=pl.BlockSpec((nb, c, w), lambda b, t: (b, t, 0)),
        out_shape=jax.ShapeDtypeStruct((bsz, seq, w), F32),
        scratch_shapes=[pltpu.VMEM((8, RWKV_COLS), F32),
                        pltpu.VMEM((nb, w // LANES, LANES, LANES), F32)],
        compiler_params=_params(("parallel", "arbitrary")),
        name="rwkv7",
    )(p_r, vec(mu), vec(w0), vec(a0), vec(k_k), vec(k_a), vec(r_k), vec(ln_g), vec(ln_b),
      wup_pad, aup_pad, gup.astype(BF16), ones)


def _sba_kernel(q_ref, kt_ref, v_ref, g_ref, cs_ref, ones_ref, o_ref, acc_ref, car_ref, *, tq, tk):
    qi = pl.program_id(1)
    nsub = tq // tk
    lane = lax.broadcasted_iota(jnp.int32, (1, LANES), 1)
    first = lane < HEAD_DIM
    cs = cs_ref[...]
    for pair in range(SBA_WIDTH // LANES):
        _sba_pair(slice(pair * LANES, (pair + 1) * LANES), qi, nsub, first, cs, q_ref, kt_ref, v_ref, g_ref,
                  ones_ref, o_ref, acc_ref, car_ref, tq, tk)


def _sba_pair(cols, qi, nsub, first, cs, q_ref, kt_ref, v_ref, g_ref, ones_ref, o_ref, acc_ref, car_ref, tq, tk):
    q = q_ref[0, :, cols]
    zero = jnp.zeros_like(q)
    qh = (jnp.where(first, q, zero), jnp.where(first, zero, q))
    acc_ref[...] = jnp.zeros_like(acc_ref)
    car_ref[...] = jnp.zeros_like(car_ref)

    def tile(j, r0, masked):
        kt = kt_ref[j, cols, :]
        vt = v_ref[0, pl.ds(pl.multiple_of(j * tk, tk), tk), cols]
        if masked:
            ti = lax.broadcasted_iota(jnp.int32, (tq - r0, tk), 0)
            si = lax.broadcasted_iota(jnp.int32, (tq - r0, tk), 1)
            mask = si < ti
        for h in range(2):
            z = _dot(qh[h][r0:], kt)
            neg_abs = lax.bitcast_convert_type(lax.bitcast_convert_type(z, jnp.int32) | SIGN_BIT, F32)
            nk = jnp.maximum(z, 0.0) + jnp.log2(1.0 + jnp.exp2(neg_abs))
            if masked:
                nk = jnp.where(mask, nk, 0.0)
            after = _dot(nk.astype(BF16), cs)
            car = car_ref[h, r0:, :]
            attn = jnp.exp2((z - nk) + after + car)
            if masked:
                attn = jnp.where(mask, attn, 0.0)
            acc_ref[h, r0:, :] += _dot(attn.astype(BF16), vt)
            car_ref[h, r0:, :] = car + (after[:, 0:1] - nk[:, 0:1])

    for d in range(nsub - 1, -1, -1):
        tile(qi * nsub + d, d * tk, True)

    def alive():
        return (jnp.max(car_ref[...]) >= SBA_DEAD_LOG2).astype(jnp.int32)

    def cond(state):
        i, live = state
        return (i < qi * nsub) & (live > 0)

    def body(state):
        i, _ = state
        tile(qi * nsub - 1 - i, 0, False)
        return i + 1, alive()

    lax.while_loop(cond, body, (jnp.int32(0), alive()))
    o = jnp.where(first, acc_ref[0], acc_ref[1])
    ms = _dot((o * o).astype(BF16), ones_ref[...]) * (1.0 / HEAD_DIM)
    o_ref[0, :, cols] = o * lax.rsqrt(ms + RMS_EPS) * g_ref[:, cols]


def _sba(q, kt, v, norm_g, tq=512):
    bsz, seq, _ = q.shape
    tk = kt.shape[2]
    tq = min(tq, seq)
    ji = jnp.arange(tk)
    cs = -(ji[:, None] > ji[None, :]).astype(BF16)
    head = jnp.arange(LANES) // HEAD_DIM
    ones = (head[:, None] == head[None, :]).astype(BF16)
    return pl.pallas_call(
        functools.partial(_sba_kernel, tq=tq, tk=tk),
        grid=(bsz, seq // tq),
        in_specs=[pl.BlockSpec((1, tq, SBA_WIDTH), lambda b, i: (b, i, 0)),
                  pl.BlockSpec((seq // tk, SBA_WIDTH, tk), lambda b, i: (b, 0, 0)),
                  pl.BlockSpec((1, seq, SBA_WIDTH), lambda b, i: (b, 0, 0)),
                  pl.BlockSpec((1, SBA_WIDTH), lambda b, i: (0, 0)),
                  pl.BlockSpec((tk, tk), lambda b, i: (0, 0)),
                  pl.BlockSpec((LANES, LANES), lambda b, i: (0, 0))],
        out_specs=pl.BlockSpec((1, tq, SBA_WIDTH), lambda b, i: (b, i, 0)),
        out_shape=jax.ShapeDtypeStruct((bsz, seq, SBA_WIDTH), F32),
        scratch_shapes=[pltpu.VMEM((2, tq, LANES), F32), pltpu.VMEM((2, tq, 1), F32)],
        compiler_params=_params(("parallel", "arbitrary")),
        name="stickbreak",
    )(q, kt, v, norm_g.reshape(1, SBA_WIDTH), cs, ones)


def _layer_norm(h, g, b):
    mu = jnp.mean(h, axis=-1, keepdims=True)
    d = h - mu
    var = jnp.mean(d * d, axis=-1, keepdims=True)
    return d * lax.rsqrt(var + LN_EPS) * g + b


def _route(scores):
    s = [scores[e:e + 1, :] for e in range(N_EXPERTS)]
    top2 = []
    gsum = []
    for grp in range(N_GROUPS):
        mem = list(range(grp * EXPERTS_PER_GROUP, (grp + 1) * EXPERTS_PER_GROUP))
        tot = None
        for e in mem:
            rank = None
            for o in mem:
                if o == e:
                    continue
                ahead = (s[o] > s[e]) | ((s[o] == s[e]) if o < e else False)
                ahead = jnp.where(ahead, 1.0, 0.0)
                rank = ahead if rank is None else rank + ahead
            sel = rank < 1.5
            top2.append(sel)
            part = jnp.where(sel, s[e], 0.0)
            tot = part if tot is None else tot + part
        gsum.append(tot)
    best = gsum[0]
    best_g = jnp.zeros_like(best)
    for grp in range(1, N_GROUPS):
        upd = gsum[grp] > best
        best = jnp.where(upd, gsum[grp], best)
        best_g = jnp.where(upd, float(grp), best_g)
    rows = []
    for k in range(EXPERTS_PER_GROUP):
        acc = None
        for grp in range(N_GROUPS):
            e = grp * EXPERTS_PER_GROUP + k
            part = jnp.where((best_g == float(grp)) & top2[e], s[e] / gsum[grp], 0.0)
            acc = part if acc is None else acc + part
        rows.append(acc)
    return jnp.concatenate(rows, axis=0), best_g


def _outproj_kernel(yr_ref, yl_ref, ys_ref, x_ref, w_ref, g_ref, b_ref, rw_ref, rb_ref,
                    x1_ref, g4_ref, grp_ref, cnt_ref):
    tm = x_ref.shape[0]
    part = tm // OUTPROJ_PARTS
    spans = [(i * part, (i + 1) * part) for i in range(OUTPROJ_PARTS)]
    r0, r1 = RWKV_WIDTH, RWKV_WIDTH + LRU_WIDTH
    x1 = []
    for a, b in spans:
        mix = (_dot(yr_ref[a:b, :].astype(BF16), w_ref[0:r0, :])
               + _dot(yl_ref[a:b, :].astype(BF16), w_ref[r0:r1, :])
               + _dot(ys_ref[a:b, :].astype(BF16), w_ref[r1:, :]))
        x1.append(_layer_norm(ALPHA * x_ref[a:b, :] + mix, g_ref[...], b_ref[...]))
        x1_ref[a:b, :] = x1[-1]
    rw_hi = rw_ref[...].astype(BF16)
    rw_lo = (rw_ref[...] - rw_hi.astype(F32)).astype(BF16)
    logits = []
    for v in x1:
        hi = v.astype(BF16)
        lo = (v - hi.astype(F32)).astype(BF16)
        logits.append(_dot_nt(rw_hi, hi) + (_dot_nt(rw_hi, lo) + _dot_nt(rw_lo, hi)) + rb_ref[...])
    lane = lax.broadcasted_iota(jnp.int32, (1, LANES), 1)
    cnt = jnp.zeros((1, LANES), F32)
    for (a, b), lg in zip(spans, logits):
        ex = jnp.exp(lg - jnp.max(lg, axis=0, keepdims=True))
        gates, best_g = _route(ex / jnp.sum(ex, axis=0, keepdims=True))
        g4_ref[:, a:b] = gates
        grp_ref[:, a:b] = best_g.astype(jnp.int32)
        for grp in range(N_GROUPS):
            n_grp = jnp.sum(jnp.where(best_g == float(grp), 1.0, 0.0), axis=1, keepdims=True)
            cnt = cnt + jnp.where(lane == grp, n_grp, 0.0)
    cnt_ref[0] = cnt.astype(jnp.int32)


def _outproj(y_r, y_l, y_s, x2, w_bf16, ln_g, ln_b, router_wt, router_b, tm=ROUTE_TILE):
    t, d = x2.shape
    tm = min(tm, t)
    row = lambda width: pl.BlockSpec((tm, width), lambda i: (i, 0))
    full = lambda shape: pl.BlockSpec(shape, lambda i: (0,) * len(shape))
    return pl.pallas_call(
        _outproj_kernel,
        grid=(t // tm,),
        in_specs=[row(RWKV_WIDTH), row(LRU_WIDTH), row(SBA_WIDTH), row(d), full((d, d)),
                  full((1, d)), full((1, d)), full((N_EXPERTS, d)), full((N_EXPERTS, 1))],
        out_specs=[row(d), pl.BlockSpec((EXPERTS_PER_GROUP, tm), lambda i: (0, i)),
                   pl.BlockSpec((1, tm), lambda i: (0, i)),
                   pl.BlockSpec((1, 1, LANES), lambda i: (i, 0, 0))],
        out_shape=[jax.ShapeDtypeStruct((t, d), F32),
                   jax.ShapeDtypeStruct((EXPERTS_PER_GROUP, t), F32),
                   jax.ShapeDtypeStruct((1, t), jnp.int32),
                   jax.ShapeDtypeStruct((t // tm, 1, LANES), jnp.int32)],
        compiler_params=_params(("parallel",)),
        name="outproj_ln_router",
    )(y_r, y_l, y_s, x2, w_bf16, ln_g.reshape(1, d), ln_b.reshape(1, d), router_wt,
      router_b.reshape(N_EXPERTS, 1))


def _moe_kernel(cnt_ref, x_ref, grp_ref, g4_ref, wg_ref, wu_ref, wd_ref, g_ref, b_ref, o_ref,
                xs_ref, gs_ref, ys_ref, dest_ref, *, tm, rb, nsub):
    i = pl.program_id(0)
    e = pl.program_id(1)
    cap = xs_ref.shape[0]
    chunks = tm // LANES
    n = [sum(cnt_ref[i * nsub + s, grp] for s in range(nsub)) for grp in range(N_GROUPS)]
    start = [0]
    for grp in range(1, N_GROUPS):
        start.append(start[-1] + n[grp - 1])

    @pl.when(e == 0)
    def _():
        grp = grp_ref[...]
        onehot = jnp.concatenate([jnp.where(grp == k, 1.0, 0.0) for k in range(N_GROUPS)], axis=0)
        li = lax.broadcasted_iota(jnp.int32, (LANES, LANES), 0)
        lj = lax.broadcasted_iota(jnp.int32, (LANES, LANES), 1)
        before = jnp.where(li < lj, 1.0, 0.0).astype(BF16)
        within = _dot(onehot.astype(BF16), before)
        tot = jnp.broadcast_to(jnp.sum(onehot, axis=1, keepdims=True), onehot.shape)
        ri = lax.broadcasted_iota(jnp.int32, (N_GROUPS * chunks, N_GROUPS * chunks), 0)
        ci = lax.broadcasted_iota(jnp.int32, (N_GROUPS * chunks, N_GROUPS * chunks), 1)
        earlier = jnp.where(((ri // chunks) == (ci // chunks)) & (ci < ri), 1.0, 0.0).astype(BF16)
        rank = within + _dot(earlier, tot.astype(BF16))
        dest = jnp.zeros((chunks, LANES), F32)
        for k in range(N_GROUPS):
            dest = jnp.where(grp == k, rank[k * chunks:(k + 1) * chunks] + jnp.asarray(start[k]).astype(F32), dest)
        dest_ref[...] = dest
        dest_i = dest.astype(jnp.int32)
        r_iota = lax.broadcasted_iota(jnp.int32, (tm, LANES), 0)
        place = jnp.concatenate(
            [jnp.where(r_iota == dest_i[c:c + 1, :], 1.0, 0.0).astype(BF16) for c in range(chunks)],
            axis=1)
        xs_ref[0:tm, :] = _dot(place, x_ref[...].astype(BF16)).astype(BF16)
        xs_ref[tm:cap, :] = jnp.zeros((cap - tm, xs_ref.shape[1]), BF16)
        pieces = [p.astype(F32) for p in _split3(g4_ref[...])]
        gates_t = jnp.transpose(jnp.concatenate(
            pieces + [jnp.zeros((LANES - 3 * EXPERTS_PER_GROUP, tm), F32)], axis=0))
        gs_ref[0:tm, :] = _dot(place, gates_t.astype(BF16))
        gs_ref[tm:cap, :] = jnp.zeros((cap - tm, LANES), F32)
        ys_ref[...] = jnp.zeros_like(ys_ref)

    e0 = e * MOE_EXPERTS_PER_STEP
    grp_e = e0 // EXPERTS_PER_GROUP
    n_e = n[0]
    s_e = start[0]
    for grp in range(1, N_GROUPS):
        n_e = jnp.where(grp_e == grp, n[grp], n_e)
        s_e = jnp.where(grp_e == grp, start[grp], s_e)
    lane = lax.broadcasted_iota(jnp.int32, (1, LANES), 1)
    s_al = s_e // MOE_ALIGN * MOE_ALIGN

    def block(b, carry):
        r0 = pl.multiple_of(s_al + b * rb, MOE_ALIGN)
        xb = xs_ref[pl.ds(r0, rb), :]
        gates = gs_ref[pl.ds(r0, rb), :]
        rid = r0 + lax.broadcasted_iota(jnp.int32, (rb, 1), 0)
        inside = (rid >= s_e) & (rid < s_e + n_e)
        acc = None
        for j in range(MOE_EXPERTS_PER_STEP):
            k_e = e0 % EXPERTS_PER_GROUP + j
            mine = ((lane % EXPERTS_PER_GROUP) == k_e) & (lane < 3 * EXPERTS_PER_GROUP)
            gate = jnp.where(inside, jnp.sum(jnp.where(mine, gates, 0.0), axis=1, keepdims=True), 0.0)
            hg = _dot(xb, wg_ref[j])
            hu = _dot(xb, wu_ref[j])
            h = hg * _sigmoid(hg) * hu * gate
            y = _dot(h.astype(BF16), wd_ref[j])
            acc = y if acc is None else acc + y
        rows = pl.ds(r0, rb)
        ys_ref[rows, :] = jnp.where(inside, acc.astype(BF16), ys_ref[rows, :])
        return carry

    lax.fori_loop(0, jnp.where(n_e > 0, (s_e + n_e - s_al + rb - 1) // rb, 0), block, 0)

    @pl.when(e == pl.num_programs(1) - 1)
    def _():
        dest = dest_ref[...]
        dest_t = jnp.transpose(jnp.concatenate([dest, jnp.zeros((LANES - chunks, LANES), F32)], axis=0))
        dest_t = dest_t.astype(jnp.int32)
        c_iota = lax.broadcasted_iota(jnp.int32, (LANES, tm), 1)
        ys = ys_ref[0:tm, :]
        for c in range(chunks):
            rows = slice(c * LANES, (c + 1) * LANES)
            unplace = jnp.where(c_iota == dest_t[:, c:c + 1], 1.0, 0.0).astype(BF16)
            o_ref[rows, :] = _layer_norm(ALPHA * x_ref[rows, :] + _dot(unplace, ys), g_ref[...], b_ref[...])


def _moe(x1, g4, grp, cnt, wg, wu, wd, ln_g, ln_b, tm=MOE_TILE, rb=MOE_ROWS):
    t, d = x1.shape
    de = wg.shape[2]
    tm = min(tm, t)
    nsub = cnt.shape[0] * tm // t
    cap = tm + rb
    eps = MOE_EXPERTS_PER_STEP
    grid_spec = pltpu.PrefetchScalarGridSpec(
        num_scalar_prefetch=1,
        grid=(t // tm, N_EXPERTS // eps),
        in_specs=[pl.BlockSpec((tm, d), lambda i, e, c: (i, 0)),
                  pl.BlockSpec((tm // LANES, LANES), lambda i, e, c: (i, 0)),
                  pl.BlockSpec((EXPERTS_PER_GROUP, tm), lambda i, e, c: (0, i)),
                  pl.BlockSpec((eps, d, de), lambda i, e, c: (e, 0, 0)),
                  pl.BlockSpec((eps, d, de), lambda i, e, c: (e, 0, 0)),
                  pl.BlockSpec((eps, de, d), lambda i, e, c: (e, 0, 0)),
                  pl.BlockSpec((1, d), lambda i, e, c: (0, 0)),
                  pl.BlockSpec((1, d), lambda i, e, c: (0, 0))],
        out_specs=pl.BlockSpec((tm, d), lambda i, e, c: (i, 0)),
        scratch_shapes=[pltpu.VMEM((cap, d), BF16), pltpu.VMEM((cap, LANES), F32),
                        pltpu.VMEM((cap, d), BF16), pltpu.VMEM((tm // LANES, LANES), F32)])
    return pl.pallas_call(
        functools.partial(_moe_kernel, tm=tm, rb=rb, nsub=nsub),
        grid_spec=grid_spec,
        out_shape=jax.ShapeDtypeStruct((t, d), F32),
        compiler_params=_params(("parallel", "arbitrary")),
        name="moe_ln",
    )(cnt.reshape(cnt.shape[0], LANES)[:, :N_GROUPS], x1, grp.reshape(t // LANES, LANES), g4,
      wg, wu, wd, ln_g.reshape(1, d), ln_b.reshape(1, d))


def kernel(x, w_in, shift_mu, rwkv_w0, rwkv_w_up, rwkv_a0, rwkv_a_up, rwkv_g_up, rwkv_k_k, rwkv_k_a, rwkv_r_k, rwkv_ln_g, rwkv_ln_b, lru_conv_w, lru_conv_b, lru_wa, lru_ba, lru_wx, lru_bx, lru_lambda, lru_norm_g, sba_norm_g, w_out, ln1_g, ln1_b, ln2_g, ln2_b, router_w, router_b, exp_w_gate, exp_w_up, exp_w_down):
    bsz, seq, d = x.shape
    t = bsz * seq
    x2 = x.reshape(t, d)
    router_wt = router_w.T
    for l in range(w_in.shape[0]):
        p_r, y_l, p_q, p_kt, p_v = _inproj(x2, seq, w_in[l], lru_conv_w[l], lru_conv_b[l], lru_wa[l], lru_ba[l],
                                            lru_wx[l], lru_bx[l], lru_lambda[l], lru_norm_g[l])
        y_r = _rwkv(p_r.reshape(bsz, seq, RWKV_COLS), shift_mu[l], rwkv_w0[l], rwkv_w_up[l],
                    rwkv_a0[l], rwkv_a_up[l], rwkv_g_up[l], rwkv_k_k[l], rwkv_k_a[l],
                    rwkv_r_k[l], rwkv_ln_g[l], rwkv_ln_b[l])
        y_s = _sba(p_q.reshape(bsz, seq, SBA_WIDTH), p_kt, p_v.reshape(bsz, seq, SBA_WIDTH), sba_norm_g[l])
        x1, g4, grp, cnt = _outproj(y_r.reshape(t, RWKV_WIDTH), y_l,
                                    y_s.reshape(t, SBA_WIDTH), x2, w_out[l].astype(BF16),
                                    ln1_g[l], ln1_b[l], router_wt, router_b)
        x2 = _moe(x1, g4, grp, cnt, exp_w_gate[l].astype(BF16), exp_w_up[l].astype(BF16),
                  exp_w_down[l].astype(BF16), ln2_g[l], ln2_b[l])
    return x2.reshape(bsz, seq, d)
```
